```python
import jax, jax.numpy as jnp
from jax import lax
import numpy as np

D_MODEL = 1024
BATCH = 4
SEQ = 4096
DEPTH = 1

GRID_W = 64
CTX_LEN = 256
RET_HEADS = 4
RET_QK = 128
RET_V = 256
RET_CHUNK = 128
RET_ROPE_BASE = 10000.0
MLA_HEADS = 8
MLA_NOPE = 64
MLA_ROPE = 32
MLA_V = 64
MLA_Q_RANK = 384
MLA_KV_RANK = 256
ROPE_BASE = 10000.0
Q_BLOCK = 128
D_FF = -(-8 * D_MODEL // (3 * 256)) * 256
IN_SPLITS = (RET_HEADS * RET_QK, RET_HEADS * RET_QK, RET_HEADS * RET_V, RET_HEADS * RET_V,
             MLA_Q_RANK, MLA_KV_RANK, MLA_ROPE, D_MODEL, D_MODEL)
IN_WIDTH = sum(IN_SPLITS)
LN_EPS = 1e-5
RMS_EPS = 1e-6

kernel_name = 'hybrid_retention_mla_prefix_dit_block'


def _layer_norm(x, g, b):
    xf = x.astype(jnp.float32)
    xc = xf - jnp.mean(xf, -1, keepdims=True)
    var = jnp.mean(xc * xc, -1, keepdims=True)
    return (xc * lax.rsqrt(var + LN_EPS) * g + b).astype(x.dtype)


def _rms_norm(x, g):
    xf = x.astype(jnp.float32)
    return (xf * lax.rsqrt(jnp.mean(xf * xf, -1, keepdims=True) + RMS_EPS) * g).astype(x.dtype)


def _head_norm(x):
    xf = x.astype(jnp.float32)
    xc = xf - jnp.mean(xf, -1, keepdims=True)
    var = jnp.mean(xc * xc, -1, keepdims=True)
    return (xc * lax.rsqrt(var + LN_EPS)).astype(x.dtype)


def _rope(x, pos, base):
    d = x.shape[-1]
    half = d // 2
    freqs = base ** (-jnp.arange(half, dtype=jnp.float32) / half)
    ang = pos[:, None] * freqs[None, :]
    cos = jnp.cos(ang)[None, :, None, :]
    sin = jnp.sin(ang)[None, :, None, :]
    xf = x.astype(jnp.float32)
    x1, x2 = xf[..., :half], xf[..., half:]
    return jnp.concatenate([x1 * cos - x2 * sin, x2 * cos + x1 * sin], -1).astype(x.dtype)


def _rope_2d(x, row, col):
    h = x.shape[-1] // 2
    return jnp.concatenate([_rope(x[..., :h], row, ROPE_BASE), _rope(x[..., h:], col, ROPE_BASE)], -1)


def _split_in(p):
    bounds = []
    acc = 0
    for w in IN_SPLITS[:-1]:
        acc += w
        bounds.append(acc)
    return jnp.split(p, bounds, axis=-1)


def _heads(a, n):
    return a.reshape(a.shape[0], a.shape[1], n, a.shape[-1] // n)


def _retention_dir(q, k, v, log_g, s0):
    B, L, H, _ = q.shape
    dv = v.shape[-1]
    C = RET_CHUNK
    N = L // C

    def chunks(a):
        return a.astype(jnp.float32).reshape(B, N, C, H, a.shape[-1]).transpose(1, 0, 3, 2, 4)

    idx = jnp.arange(C, dtype=jnp.float32)
    rel = idx[:, None] - idx[None, :]
    inner = jnp.where(rel[None] >= 0, jnp.exp(jnp.maximum(rel, 0.0)[None] * log_g[:, None, None]), 0.0)
    q_dec = jnp.exp((idx + 1.0)[None, :] * log_g[:, None])[None, :, :, None]
    k_dec = jnp.exp((C - 1.0 - idx)[None, :] * log_g[:, None])[None, :, :, None]
    c_dec = jnp.exp(C * log_g)[None, :, None, None]

    def step(s, blk):
        qc, kc, vc = blk
        a = jnp.einsum('bhid,bhjd->bhij', qc, kc) * inner
        o = jnp.einsum('bhij,bhjv->bhiv', a, vc) + jnp.einsum('bhid,bhdv->bhiv', qc, s) * q_dec
        s = s * c_dec + jnp.einsum('bhjd,bhjv->bhdv', kc * k_dec, vc)
        return s, o

    s, o = lax.scan(step, s0, (chunks(q), chunks(k), chunks(v)))
    out = o.transpose(1, 0, 3, 2, 4).reshape(B, L, H, dv)
    return out.astype(v.dtype), s


def _attend(q, k, v):
    s = jnp.einsum('bqhd,bkhd->bhqk', q, k).astype(jnp.float32) * (q.shape[-1] ** -0.5)
    p = jax.nn.softmax(s, axis=-1).astype(v.dtype)
    return jnp.einsum('bhqk,bkhv->bqhv', p, v)


def _blocked_attend(q, k, v):
    B, L, H, d = q.shape
    nb = L // Q_BLOCK
    qb = q.reshape(B, nb, Q_BLOCK, H, d).transpose(1, 0, 2, 3, 4)
    ob = lax.map(lambda qi: _attend(qi, k, v), qb)
    return ob.transpose(1, 0, 2, 3, 4).reshape(B, L, H, ob.shape[-1])


def _mla_qkv(dq, dkv, kr, q_norm, w_uq, kv_norm, w_ukv, pos2d):
    B, L = dq.shape[0], dq.shape[1]
    q = _heads(_rms_norm(dq, q_norm) @ w_uq, MLA_HEADS)
    kv = _heads(_rms_norm(dkv, kv_norm) @ w_ukv, MLA_HEADS)
    q_nope, q_rope = q[..., :MLA_NOPE], q[..., MLA_NOPE:]
    k_nope, v = kv[..., :MLA_NOPE], kv[..., MLA_NOPE:]
    k_rope = kr[:, :, None, :]
    if pos2d is not None:
        q_rope = _rope_2d(q_rope, pos2d[0], pos2d[1])
        k_rope = _rope_2d(k_rope, pos2d[0], pos2d[1])
    k_rope = jnp.broadcast_to(k_rope, (B, L, MLA_HEADS, MLA_ROPE))
    return (jnp.concatenate([q_nope, q_rope], -1), jnp.concatenate([k_nope, k_rope], -1), v)


def _swiglu(h, w_gu, w_down):
    a, b = jnp.split(h @ w_gu, 2, axis=-1)
    return (jax.nn.silu(a) * b) @ w_down


def _token_mixer(hc, hl, row, col, tpos, w_in, dec_f, dec_b, w_ret_o, q_norm, w_uq, kv_norm, w_ukv,
                 w_mla_o, w_out, need_ctx):
    B, L, _ = hl.shape
    rq_c, rk_c, rv_c, rg_c, dq_c, dkv_c, kr_c, gr_c, gm_c = _split_in(hc @ w_in)
    rq_l, rk_l, rv_l, rg_l, dq_l, dkv_l, kr_l, gr_l, gm_l = _split_in(hl @ w_in)

    k_scale = RET_QK ** -0.5
    qc = _heads(rq_c, RET_HEADS)
    kc = _heads(rk_c, RET_HEADS) * k_scale
    vc = _heads(rv_c, RET_HEADS)
    ql = _rope(_heads(rq_l, RET_HEADS), tpos, RET_ROPE_BASE)
    kl = _rope(_heads(rk_l, RET_HEADS), tpos, RET_ROPE_BASE) * k_scale
    vl = _heads(rv_l, RET_HEADS)
    lg_f = jax.nn.log_sigmoid(dec_f.astype(jnp.float32))
    lg_b = jax.nn.log_sigmoid(dec_b.astype(jnp.float32))
    s0 = jnp.zeros((B, RET_HEADS, RET_QK, RET_V), jnp.float32)
    flip = lambda a: jnp.flip(a, axis=1)
    oc_f, s_f = _retention_dir(qc, kc, vc, lg_f, s0)
    oc_b, s_b = _retention_dir(flip(qc), flip(kc), flip(vc), lg_b, s0)
    ol_f, _ = _retention_dir(ql, kl, vl, lg_f, s_f)
    ol_b, _ = _retention_dir(flip(ql), flip(kl), flip(vl), lg_b, s_b)
    ret_l = (jax.nn.silu(rg_l) * _head_norm(ol_f + flip(ol_b)).reshape(B, L, -1)) @ w_ret_o

    q_c, k_c, v_c = _mla_qkv(dq_c, dkv_c, kr_c, q_norm, w_uq, kv_norm, w_ukv, None)
    q_l, k_l, v_l = _mla_qkv(dq_l, dkv_l, kr_l, q_norm, w_uq, kv_norm, w_ukv, (row, col))
    k_all = jnp.concatenate([k_c, k_l], axis=1)
    v_all = jnp.concatenate([v_c, v_l], axis=1)
    mla_l = _blocked_attend(q_l, k_all, v_all).reshape(B, L, -1) @ w_mla_o

    y_l = (jax.nn.sigmoid(gr_l) * ret_l + jax.nn.sigmoid(gm_l) * mla_l) @ w_out

    y_c = None
    if need_ctx:
        Lc = hc.shape[1]
        ret_c = (jax.nn.silu(rg_c) * _head_norm(oc_f + flip(oc_b)).reshape(B, Lc, -1)) @ w_ret_o
        mla_c = _attend(q_c, k_c, v_c).reshape(B, Lc, -1) @ w_mla_o
        y_c = (jax.nn.sigmoid(gr_c) * ret_c + jax.nn.sigmoid(gm_c) * mla_c) @ w_out
    return y_c, y_l


def setup_inputs(seed: int = 0) -> dict:
    key = jax.random.key(seed)
    ks = jax.random.split(key, 22)
    f32 = jnp.float32
    beta = (8.0 * DEPTH) ** -0.25

    def nrm(k, shape, scale):
        return jax.random.normal(k, shape, f32) * scale

    dec0 = jnp.log(jnp.exp2(5.0 + jnp.arange(RET_HEADS, dtype=f32)) - 1.0)
    return {
        'x': nrm(ks[0], (BATCH, SEQ, D_MODEL), 1.0),
        'c': nrm(ks[1], (BATCH, D_MODEL), 1.0),
        'ctx': nrm(ks[2], (BATCH, CTX_LEN, D_MODEL), 1.0),
        'c_ctx': nrm(ks[3], (D_MODEL,), 1.0),
        'w_ada': nrm(ks[4], (DEPTH, D_MODEL, 6 * D_MODEL), D_MODEL ** -0.5),
        'b_ada': nrm(ks[5], (DEPTH, 6 * D_MODEL), 0.02),
        'w_in': nrm(ks[6], (DEPTH, D_MODEL, IN_WIDTH), D_MODEL ** -0.5),
        'ret_decay_f': dec0 + nrm(ks[7], (DEPTH, RET_HEADS), 0.1),
        'ret_decay_b': dec0 + nrm(ks[8], (DEPTH, RET_HEADS), 0.1),
        'w_ret_o': nrm(ks[9], (DEPTH, RET_HEADS * RET_V, D_MODEL), beta * (RET_HEADS * RET_V) ** -0.5),
        'mla_q_norm': 1.0 + nrm(ks[10], (DEPTH, MLA_Q_RANK), 0.02),
        'w_uq': nrm(ks[11], (DEPTH, MLA_Q_RANK, MLA_HEADS * (MLA_NOPE + MLA_ROPE)), MLA_Q_RANK ** -0.5),
        'mla_kv_norm': 1.0 + nrm(ks[12], (DEPTH, MLA_KV_RANK), 0.02),
        'w_ukv': nrm(ks[13], (DEPTH, MLA_KV_RANK, MLA_HEADS * (MLA_NOPE + MLA_V)), MLA_KV_RANK ** -0.5),
        'w_mla_o': nrm(ks[14], (DEPTH, MLA_HEADS * MLA_V, D_MODEL), beta * (MLA_HEADS * MLA_V) ** -0.5),
        'w_out': nrm(ks[15], (DEPTH, D_MODEL, D_MODEL), beta * D_MODEL ** -0.5),
        'ln1_g': 1.0 + nrm(ks[16], (DEPTH, D_MODEL), 0.02),
        'ln1_b': nrm(ks[17], (DEPTH, D_MODEL), 0.02),
        'w_gu': nrm(ks[18], (DEPTH, D_MODEL, 2 * D_FF), D_MODEL ** -0.5),
        'w_down': nrm(ks[19], (DEPTH, D_FF, D_MODEL), beta * D_FF ** -0.5),
        'ln2_g': 1.0 + nrm(ks[20], (DEPTH, D_MODEL), 0.02),
        'ln2_b': nrm(ks[21], (DEPTH, D_MODEL), 0.02),
    }


def reference(x, c, ctx, c_ctx, w_ada, b_ada, w_in, ret_decay_f, ret_decay_b, w_ret_o, mla_q_norm, w_uq,
              mla_kv_norm, w_ukv, w_mla_o, w_out, ln1_g, ln1_b, w_gu, w_down, ln2_g, ln2_b):
    L = x.shape[1]
    rows = L // GRID_W
    row = jnp.repeat(jnp.arange(rows, dtype=jnp.float32), GRID_W)
    col = jnp.tile(jnp.arange(GRID_W, dtype=jnp.float32), rows)
    tpos = jnp.arange(L, dtype=jnp.float32)
    alpha = (2.0 * DEPTH) ** 0.25
    sc = jax.nn.silu(c)
    scc = jax.nn.silu(c_ctx)
    for i in range(DEPTH):
        last = i == DEPTH - 1
        sh1, s1, g1, sh2, s2, g2 = jnp.split((sc @ w_ada[i] + b_ada[i])[:, None, :], 6, axis=-1)
        csh1, cs1, cg1, csh2, cs2, cg2 = jnp.split(scc @ w_ada[i] + b_ada[i], 6, axis=-1)
        hl = x * (1.0 + s1) + sh1
        hc = ctx * (1.0 + cs1) + csh1
        y_c, y_l = _token_mixer(hc, hl, row, col, tpos, w_in[i], ret_decay_f[i], ret_decay_b[i], w_ret_o[i],
                                mla_q_norm[i], w_uq[i], mla_kv_norm[i], w_ukv[i], w_mla_o[i], w_out[i],
                                not last)
        x = _layer_norm(alpha * x + g1 * y_l, ln1_g[i], ln1_b[i])
        x = _layer_norm(alpha * x + g2 * _swiglu(x * (1.0 + s2) + sh2, w_gu[i], w_down[i]), ln2_g[i], ln2_b[i])
        if not last:
            ctx = _layer_norm(alpha * ctx + cg1 * y_c, ln1_g[i], ln1_b[i])
            ctx = _layer_norm(alpha * ctx + cg2 * _swiglu(ctx * (1.0 + cs2) + csh2, w_gu[i], w_down[i]),
                              ln2_g[i], ln2_b[i])
    return x
```

```python
import functools
import math

import numpy as np
import jax
import jax.numpy as jnp
from jax import lax
from jax.experimental import pallas as pl
from jax.experimental.pallas import tpu as pltpu

F32 = jnp.float32
BF16 = jnp.bfloat16

D_MODEL = 1024
GRID_W = 64
RET_HEADS = 4
RET_QK = 128
RET_V = 256
RET_CHUNK = 128
RET_ROPE_BASE = 10000.0
MLA_HEADS = 8
MLA_NOPE = 64
MLA_ROPE = 32
MLA_V = 64
MLA_Q_RANK = 384
MLA_KV_RANK = 256
ROPE_BASE = 10000.0
D_FF = -(-8 * D_MODEL // (3 * 256)) * 256
LN_EPS = 1e-5
RMS_EPS = 1e-6
DEPTH = 1
ALPHA = (2.0 * DEPTH) ** 0.25

LANES = 128
MLA_HEAD_PAD = LANES
VMEM_LIMIT = 56 * 1024 * 1024

_RQ = RET_HEADS * RET_QK
_RV = RET_HEADS * RET_V
_COLS = {}
_off = 0
for _name, _w in (("rq", _RQ), ("rk", _RQ), ("rv", _RV), ("rg", _RV), ("gr", D_MODEL), ("gm", D_MODEL),
                  ("dq", MLA_Q_RANK), ("dkv", MLA_KV_RANK), ("kr", LANES)):
    _COLS[_name] = (_off, _off + _w)
    _off += _w
IN_WIDTH_PAD = _off

_ROPE_X1 = 0
_NOPE_A = 16
_ROPE_X2 = 64
_NOPE_B = 80
_NOPE_SPLIT = 48
_RH = MLA_ROPE // 4


def _dot(a, b):
    return jnp.dot(a, b, preferred_element_type=F32)


def _dot_nt(a, b):
    return lax.dot_general(a, b, (((1,), (1,)), ((), ())), preferred_element_type=F32)


def _dot_tn(a, b):
    return lax.dot_general(a, b, (((0,), (0,)), ((), ())), preferred_element_type=F32)


def _sigmoid(x):
    return 1.0 / (1.0 + jnp.exp(-x))


def _silu(x):
    return x * _sigmoid(x)


def _layer_norm(x, g, b):
    mu = jnp.mean(x, axis=-1, keepdims=True)
    xc = x - mu
    var = jnp.mean(xc * xc, axis=-1, keepdims=True)
    return xc * lax.rsqrt(var + LN_EPS) * g + b


def _rope_half_roll(x, cos, sin_signed):
    return x * cos + pltpu.roll(x, LANES // 2, 1) * sin_signed


def _ada_kernel(c_ref, w_ref, b_ref, o_ref):
    c = c_ref[...]
    sc = _silu(c)
    o_ref[...] = jnp.dot(sc, w_ref[...], preferred_element_type=F32,
                         precision=lax.Precision.HIGHEST) + b_ref[...]


def _ada(c_rows, w_ada, b_ada):
    rows, d = c_rows.shape
    n = w_ada.shape[1]
    tn = 1536
    return pl.pallas_call(
        _ada_kernel,
        grid=(n // tn,),
        in_specs=[pl.BlockSpec((rows, d), lambda j: (0, 0)),
                  pl.BlockSpec((d, tn), lambda j: (0, j)),
                  pl.BlockSpec((1, tn), lambda j: (0, j))],
        out_specs=pl.BlockSpec((rows, tn), lambda j: (0, j)),
        out_shape=jax.ShapeDtypeStruct((rows, n), F32),
        compiler_params=pltpu.CompilerParams(dimension_semantics=("arbitrary",), vmem_limit_bytes=VMEM_LIMIT),
        name="ada",
    )(c_rows, w_ada, b_ada.reshape(1, n))


def _in_proj_kernel(*refs, is_ctx, q_scale):
    if is_ctx:
        (x_ref, mod_ref, w_ref, kvn_ref, wukv_ref,
         rk_ref, rv_ref, km_ref, vm_ref) = refs
    else:
        (x_ref, mod_ref, w_ref, qn_ref, wuq_ref, kvn_ref, wukv_ref, cr_ref, sr_ref, cm_ref, sm_ref,
         rq_ref, rk_ref, rv_ref, srg_ref, sgr_ref, sgm_ref, qm_ref, km_ref, vm_ref) = refs

    x = x_ref[0]
    sh1 = mod_ref[0, 0:1, :]
    s1 = mod_ref[0, 1:2, :]
    h = (x * (1.0 + s1) + sh1).astype(BF16)

    def proj(name):
        lo, hi = _COLS[name]
        return _dot(h, w_ref[:, lo:hi])

    k_scale = RET_QK ** -0.5
    rk = proj("rk")
    if is_ctx:
        rk_ref[0] = (rk * k_scale).astype(BF16)
    else:
        cr = cr_ref[...]
        sr = sr_ref[...]
        rq = proj("rq")
        for hh in range(RET_HEADS):
            sl = slice(hh * RET_QK, (hh + 1) * RET_QK)
            rq_ref[0, :, sl] = _rope_half_roll(rq[:, sl], cr, sr).astype(BF16)
            rk_ref[0, :, sl] = (_rope_half_roll(rk[:, sl], cr, sr) * k_scale).astype(BF16)
        srg_ref[0] = _silu(proj("rg")).astype(BF16)
        sgr_ref[0] = _sigmoid(proj("gr")).astype(BF16)
        sgm_ref[0] = _sigmoid(proj("gm")).astype(BF16)
    rv_ref[0] = proj("rv").astype(BF16)

    def rms(v, g):
        return v * lax.rsqrt(jnp.mean(v * v, axis=-1, keepdims=True) + RMS_EPS) * g

    dkv = rms(proj("dkv"), kvn_ref[...]).astype(BF16)
    kv = _dot(dkv, wukv_ref[...])
    kw = MLA_HEADS * MLA_HEAD_PAD
    vm_ref[0] = kv[:, kw:].astype(BF16)
    kr = proj("kr")
    if not is_ctx:
        cm = cm_ref[...]
        sm = sm_ref[...]
        kr = _rope_half_roll(kr, cm, sm)
        dq = rms(proj("dq"), qn_ref[...]).astype(BF16)
        q = _dot(dq, wuq_ref[...])
    for hh in range(MLA_HEADS):
        sl = slice(hh * MLA_HEAD_PAD, (hh + 1) * MLA_HEAD_PAD)
        km_ref[0, :, sl] = (kv[:, sl] + kr).astype(BF16)
        if not is_ctx:
            qm_ref[0, :, sl] = (_rope_half_roll(q[:, sl], cm, sm) * q_scale).astype(BF16)


def _in_proj(x, mod, w_in_p, qn, wuq_p, kvn, wukv_p, tables, *, is_ctx, tm):
    B, L, D = x.shape
    q_scale = (MLA_NOPE + MLA_ROPE) ** -0.5 * math.log2(math.e)
    kw = MLA_HEADS * MLA_HEAD_PAD
    vw = MLA_HEADS * MLA_V

    def const(shape):
        return pl.BlockSpec(shape, lambda b, i: (0,) * len(shape))

    def rows(width):
        return pl.BlockSpec((1, tm, width), lambda b, i: (b, i, 0))

    def out(width):
        return jax.ShapeDtypeStruct((B, L, width), BF16)

    x_spec = rows(D)
    if is_ctx:
        mod_spec = pl.BlockSpec((1, 6, D), lambda b, i: (0, 0, 0))
        in_specs = [x_spec, mod_spec, const(w_in_p.shape), const(kvn.shape), const(wukv_p.shape)]
        args = (x, mod, w_in_p, kvn, wukv_p)
        out_specs = [rows(_RQ), rows(_RV), rows(kw), rows(vw)]
        out_shape = [out(_RQ), out(_RV), out(kw), out(vw)]
    else:
        mod_spec = pl.BlockSpec((1, 6, D), lambda b, i: (b, 0, 0))
        tab = pl.BlockSpec((tm, LANES), lambda b, i: (i, 0))
        in_specs = [x_spec, mod_spec, const(w_in_p.shape), const(qn.shape), const(wuq_p.shape),
                    const(kvn.shape), const(wukv_p.shape), tab, tab, tab, tab]
        args = (x, mod, w_in_p, qn, wuq_p, kvn, wukv_p) + tuple(tables)
        out_specs = [rows(_RQ), rows(_RQ), rows(_RV), rows(_RV), rows(D), rows(D), rows(kw), rows(kw), rows(vw)]
        out_shape = [out(_RQ), out(_RQ), out(_RV), out(_RV), out(D), out(D), out(kw), out(kw), out(vw)]
    return pl.pallas_call(
        functools.partial(_in_proj_kernel, is_ctx=is_ctx, q_scale=q_scale),
        grid=(B, L // tm),
        in_specs=in_specs,
        out_specs=out_specs,
        out_shape=out_shape,
        compiler_params=pltpu.CompilerParams(dimension_semantics=("arbitrary", "arbitrary"),
                                             vmem_limit_bytes=VMEM_LIMIT),
        name="in_proj_ctx" if is_ctx else "in_proj",
    )(*args)


def _log_sigmoid(x):
    return jnp.minimum(x, 0.0) - jnp.log(1.0 + jnp.exp(-jnp.abs(x)))


def _retention_kernel(dec_ref, q_ref, k_ref, v_ref, kc_ref, vc_ref, o_ref, sf_ref, sb_ref, *, n_lat, n_ctx):
    C = RET_CHUNK
    lgf = _log_sigmoid(dec_ref[0, 0:1, :])
    lgb = _log_sigmoid(dec_ref[0, 1:2, :])
    ii = lax.broadcasted_iota(jnp.int32, (C, C), 0).astype(F32)
    jj = lax.broadcasted_iota(jnp.int32, (C, C), 1).astype(F32)
    rel = ii - jj
    mask = jnp.where(rel > 0, jnp.exp(rel * lgf), jnp.where(rel < 0, jnp.exp(-rel * lgb), 2.0))
    pos = ii[:, 0:1]
    lgf1 = lgf[:, 0:1]
    lgb1 = lgb[:, 0:1]
    qdec_f = jnp.exp((pos + 1.0) * lgf1)
    qdec_b = jnp.exp((C - pos) * lgb1)
    kdec_f = jnp.exp((C - 1.0 - pos) * lgf1)
    kdec_b = jnp.exp(pos * lgb1)
    cdec_f = jnp.exp(C * lgf1)
    cdec_b = jnp.exp(C * lgb1)

    def kv_update(kref, vref, n, kdec):
        k = kref[0, pl.ds(n * C, C), :].astype(F32)
        v = vref[0, pl.ds(n * C, C), :]
        return _dot_tn((k * kdec).astype(BF16), v)

    zero = jnp.zeros((RET_QK, RET_V), F32)
    s_f = lax.fori_loop(0, n_ctx, lambda n, s: s * cdec_f + kv_update(kc_ref, vc_ref, n, kdec_f), zero)
    s_b = lax.fori_loop(0, n_ctx,
                        lambda t, s: s * cdec_b + kv_update(kc_ref, vc_ref, n_ctx - 1 - t, kdec_b), zero)

    def fwd_state(n, s):
        sf_ref[n] = s.astype(BF16)
        return s * cdec_f + kv_update(k_ref, v_ref, n, kdec_f)

    def bwd_state(t, s):
        n = n_lat - 1 - t
        sb_ref[n] = s.astype(BF16)
        return s * cdec_b + kv_update(k_ref, v_ref, n, kdec_b)

    lax.fori_loop(0, n_lat, fwd_state, s_f)
    lax.fori_loop(0, n_lat, bwd_state, s_b)

    def out_chunk(n, carry):
        rows = pl.ds(n * C, C)
        q = q_ref[0, rows, :]
        k = k_ref[0, rows, :]
        v = v_ref[0, rows, :]
        a = (_dot_nt(q, k) * mask).astype(BF16)
        o = _dot(a, v) + _dot(q, sf_ref[n]) * qdec_f + _dot(q, sb_ref[n]) * qdec_b
        mu = jnp.mean(o, axis=-1, keepdims=True)
        oc = o - mu
        var = jnp.mean(oc * oc, axis=-1, keepdims=True)
        o_ref[0, rows, :] = (oc * lax.rsqrt(var + LN_EPS)).astype(BF16)
        return carry

    lax.fori_loop(0, n_lat, out_chunk, 0)


def _retention(dec, rq, rk, rv, rk_c, rv_c):
    B, L, _ = rq.shape
    Lc = rk_c.shape[1]
    n_lat, n_ctx = L // RET_CHUNK, Lc // RET_CHUNK
    return pl.pallas_call(
        functools.partial(_retention_kernel, n_lat=n_lat, n_ctx=n_ctx),
        grid=(B, RET_HEADS),
        in_specs=[pl.BlockSpec((1, 2, LANES), lambda b, h: (h, 0, 0)),
                  pl.BlockSpec((1, L, RET_QK), lambda b, h: (b, 0, h)),
                  pl.BlockSpec((1, L, RET_QK), lambda b, h: (b, 0, h)),
                  pl.BlockSpec((1, L, RET_V), lambda b, h: (b, 0, h)),
                  pl.BlockSpec((1, Lc, RET_QK), lambda b, h: (b, 0, h)),
                  pl.BlockSpec((1, Lc, RET_V), lambda b, h: (b, 0, h))],
        out_specs=pl.BlockSpec((1, L, RET_V), lambda b, h: (b, 0, h)),
        out_shape=jax.ShapeDtypeStruct((B, L, RET_HEADS * RET_V), BF16),
        scratch_shapes=[pltpu.VMEM((n_lat, RET_QK, RET_V), BF16),
                        pltpu.VMEM((n_lat, RET_QK, RET_V), BF16)],
        compiler_params=pltpu.CompilerParams(dimension_semantics=("arbitrary", "arbitrary"),
                                             vmem_limit_bytes=VMEM_LIMIT),
        name="retention",
    )(dec, rq, rk, rv, rk_c, rv_c)


ATT_KEY_TILE = 512


def _attention_kernel(q_ref, k_ref, v_ref, kc_ref, vc_ref, o_ref, s_ref, *, l_lat, l_ctx):
    tq = q_ref.shape[1]
    tk = ATT_KEY_TILE
    lane = lax.broadcasted_iota(jnp.int32, (tq, LANES), 1)
    out = jnp.zeros((tq, LANES), F32)
    for hh in range(2):
        hs = slice(hh * MLA_HEAD_PAD, (hh + 1) * MLA_HEAD_PAD)
        q = q_ref[0, :, hs]

        s_c = _dot_nt(q, kc_ref[0, :, hs])
        s_ref[:, 0:l_ctx] = s_c
        m = s_c[:, 0:LANES]
        for j in range(1, l_ctx // LANES):
            m = jnp.maximum(m, s_c[:, j * LANES:(j + 1) * LANES])

        def score_tile(t, m):
            s = _dot_nt(q, k_ref[0, pl.ds(t * tk, tk), hs])
            s_ref[:, pl.ds(l_ctx + t * tk, tk)] = s
            for j in range(tk // LANES):
                m = jnp.maximum(m, s[:, j * LANES:(j + 1) * LANES])
            return m

        m = lax.fori_loop(0, l_lat // tk, score_tile, m)
        m = jnp.max(m, axis=-1, keepdims=True)

        def pv_tile(s, v, l, acc):
            p = jnp.exp2(s - m)
            for j in range(p.shape[1] // LANES):
                l = l + p[:, j * LANES:(j + 1) * LANES]
            return l, acc + _dot(p.astype(BF16), v)

        l = jnp.zeros((tq, LANES), F32)
        acc = jnp.zeros((tq, LANES), F32)
        l, acc = pv_tile(s_ref[:, 0:l_ctx], vc_ref[0], l, acc)

        def pv_step(t, carry):
            l, acc = carry
            return pv_tile(s_ref[:, pl.ds(l_ctx + t * tk, tk)], v_ref[0, pl.ds(t * tk, tk), :], l, acc)

        l, acc = lax.fori_loop(0, l_lat // tk, pv_step, (l, acc))
        res = acc / jnp.sum(l, axis=-1, keepdims=True)
        out = jnp.where((lane >= hh * MLA_V) & (lane < (hh + 1) * MLA_V), res, out)
    o_ref[0] = out.astype(BF16)


def _attention(qm, km, vm, km_c, vm_c, *, tq):
    B, L, _ = qm.shape
    Lc = km_c.shape[1]
    pairs = MLA_HEADS // 2
    qk_w = 2 * MLA_HEAD_PAD
    v_w = 2 * MLA_V
    return pl.pallas_call(
        functools.partial(_attention_kernel, l_lat=L, l_ctx=Lc),
        grid=(B, pairs, L // tq),
        in_specs=[pl.BlockSpec((1, tq, qk_w), lambda b, p, i: (b, i, p)),
                  pl.BlockSpec((1, L, qk_w), lambda b, p, i: (b, 0, p)),
                  pl.BlockSpec((1, L, v_w), lambda b, p, i: (b, 0, p)),
                  pl.BlockSpec((1, Lc, qk_w), lambda b, p, i: (b, 0, p)),
                  pl.BlockSpec((1, Lc, v_w), lambda b, p, i: (b, 0, p))],
        out_specs=pl.BlockSpec((1, tq, v_w), lambda b, p, i: (b, i, p)),
        out_shape=jax.ShapeDtypeStruct((B, L, MLA_HEADS * MLA_V), BF16),
        scratch_shapes=[pltpu.VMEM((tq, Lc + L), F32)],
        compiler_params=pltpu.CompilerParams(dimension_semantics=("arbitrary", "arbitrary", "arbitrary"),
                                             vmem_limit_bytes=VMEM_LIMIT),
        name="attention",
    )(qm, km, vm, km_c, vm_c)


def _merge_kernel(x_ref, mod_ref, o_ret_ref, srg_ref, att_ref, sgr_ref, sgm_ref,
                  wro_ref, wmo_ref, wout_ref, g_ref, b_ref, y_ref):
    x = x_ref[0]
    g1 = mod_ref[0, 2:3, :]
    ret_in = (srg_ref[0].astype(F32) * o_ret_ref[0].astype(F32)).astype(BF16)
    ret = _dot(ret_in, wro_ref[...])
    mla = _dot(att_ref[0], wmo_ref[...])
    mix = (sgr_ref[0].astype(F32) * ret + sgm_ref[0].astype(F32) * mla).astype(BF16)
    y = _dot(mix, wout_ref[...])
    y_ref[0] = _layer_norm(ALPHA * x + g1 * y, g_ref[...], b_ref[...])


def _merge(x, mod, o_ret, srg, att, sgr, sgm, wro, wmo, wout, ln_g, ln_b, *, tm):
    B, L, D = x.shape

    def const(shape):
        return pl.BlockSpec(shape, lambda b, i: (0,) * len(shape))

    def rows(width):
        return pl.BlockSpec((1, tm, width), lambda b, i: (b, i, 0))

    return pl.pallas_call(
        _merge_kernel,
        grid=(B, L // tm),
        in_specs=[rows(D), pl.BlockSpec((1, 6, D), lambda b, i: (b, 0, 0)),
                  rows(o_ret.shape[2]), rows(srg.shape[2]), rows(att.shape[2]), rows(D), rows(D),
                  const(wro.shape), const(wmo.shape), const(wout.shape), const(ln_g.shape), const(ln_b.shape)],
        out_specs=rows(D),
        out_shape=jax.ShapeDtypeStruct((B, L, D), F32),
        compiler_params=pltpu.CompilerParams(dimension_semantics=("arbitrary", "arbitrary"),
                                             vmem_limit_bytes=VMEM_LIMIT),
        name="merge",
    )(x, mod, o_ret, srg, att, sgr, sgm, wro, wmo, wout, ln_g, ln_b)


def _ffn_kernel(x_ref, mod_ref, wgu_ref, wdn_ref, g_ref, b_ref, y_ref):
    x = x_ref[0]
    sh2 = mod_ref[0, 3:4, :]
    s2 = mod_ref[0, 4:5, :]
    g2 = mod_ref[0, 5:6, :]
    h = (x * (1.0 + s2) + sh2).astype(BF16)
    a = _dot(h, wgu_ref[:, 0:D_FF])
    b = _dot(h, wgu_ref[:, D_FF:2 * D_FF])
    act = (_silu(a) * b).astype(BF16)
    f = _dot(act, wdn_ref[...])
    y_ref[0] = _layer_norm(ALPHA * x + g2 * f, g_ref[...], b_ref[...])


def _ffn(x, mod, wgu, wdn, ln_g, ln_b, *, tm):
    B, L, D = x.shape

    def const(shape):
        return pl.BlockSpec(shape, lambda b, i: (0,) * len(shape))

    rows = pl.BlockSpec((1, tm, D), lambda b, i: (b, i, 0))
    return pl.pallas_call(
        _ffn_kernel,
        grid=(B, L // tm),
        in_specs=[rows, pl.BlockSpec((1, 6, D), lambda b, i: (b, 0, 0)),
                  const(wgu.shape), const(wdn.shape), const(ln_g.shape), const(ln_b.shape)],
        out_specs=rows,
        out_shape=jax.ShapeDtypeStruct((B, L, D), F32),
        compiler_params=pltpu.CompilerParams(dimension_semantics=("arbitrary", "arbitrary"),
                                             vmem_limit_bytes=VMEM_LIMIT),
        name="ffn",
    )(x, mod, wgu, wdn, ln_g, ln_b)


def _rope_tables(L):
    t = np.arange(L, dtype=np.float64)
    half = RET_QK // 2
    fr = RET_ROPE_BASE ** (-np.arange(half, dtype=np.float64) / half)
    ang = t[:, None] * fr[None, :]
    cos_r = np.concatenate([np.cos(ang), np.cos(ang)], -1)
    sin_r = np.concatenate([-np.sin(ang), np.sin(ang)], -1)

    row = np.floor(t / GRID_W)
    col = t - row * GRID_W
    fm = ROPE_BASE ** (-np.arange(_RH, dtype=np.float64) / _RH)
    ang_rc = np.concatenate([row[:, None] * fm[None, :], col[:, None] * fm[None, :]], -1)
    cos_m = np.ones((L, LANES))
    sin_m = np.zeros((L, LANES))
    cos_m[:, _ROPE_X1:_ROPE_X1 + 2 * _RH] = np.cos(ang_rc)
    cos_m[:, _ROPE_X2:_ROPE_X2 + 2 * _RH] = np.cos(ang_rc)
    sin_m[:, _ROPE_X1:_ROPE_X1 + 2 * _RH] = -np.sin(ang_rc)
    sin_m[:, _ROPE_X2:_ROPE_X2 + 2 * _RH] = np.sin(ang_rc)
    return tuple(jnp.asarray(a, F32) for a in (cos_r, sin_r, cos_m, sin_m))


def _rope_cols_padded(w_rope):
    r = _RH
    x1 = jnp.concatenate([w_rope[:, 0:r], w_rope[:, 2 * r:3 * r]], axis=1)
    x2 = jnp.concatenate([w_rope[:, r:2 * r], w_rope[:, 3 * r:4 * r]], axis=1)
    return x1, x2


def _mla_head_cols(nope, rope):
    k = (nope if nope is not None else rope).shape[0]
    z = lambda n: jnp.zeros((k, n), F32)
    if rope is None:
        x1, x2 = z(2 * _RH), z(2 * _RH)
    else:
        x1, x2 = _rope_cols_padded(rope)
    if nope is None:
        na, nb = z(_NOPE_SPLIT), z(MLA_NOPE - _NOPE_SPLIT)
    else:
        na, nb = nope[:, :_NOPE_SPLIT], nope[:, _NOPE_SPLIT:]
    return jnp.concatenate([x1, na, x2, nb, z(LANES - _NOPE_B - (MLA_NOPE - _NOPE_SPLIT))], axis=1)


def _prep_weights(w_in, w_uq, w_ukv):
    c = 0
    parts = {}
    for name, w in (("rq", _RQ), ("rk", _RQ), ("rv", _RV), ("rg", _RV), ("dq", MLA_Q_RANK),
                    ("dkv", MLA_KV_RANK), ("kr", MLA_ROPE), ("gr", D_MODEL), ("gm", D_MODEL)):
        parts[name] = w_in[:, c:c + w]
        c += w
    parts["kr"] = _mla_head_cols(None, parts["kr"])
    w_in_p = jnp.concatenate([parts[n] for n in _COLS], axis=1).astype(BF16)

    hq = MLA_NOPE + MLA_ROPE
    wuq_p = jnp.concatenate(
        [_mla_head_cols(w_uq[:, h * hq:h * hq + MLA_NOPE], w_uq[:, h * hq + MLA_NOPE:(h + 1) * hq])
         for h in range(MLA_HEADS)], axis=1).astype(BF16)
    hkv = MLA_NOPE + MLA_V
    wuk_p = jnp.concatenate([_mla_head_cols(w_ukv[:, h * hkv:h * hkv + MLA_NOPE], None)
                             for h in range(MLA_HEADS)], axis=1)
    wuv = jnp.concatenate([w_ukv[:, h * hkv + MLA_NOPE:(h + 1) * hkv] for h in range(MLA_HEADS)], axis=1)
    wukv_p = jnp.concatenate([wuk_p, wuv], axis=1).astype(BF16)
    return w_in_p, wuq_p, wukv_p


def kernel(x, c, ctx, c_ctx, w_ada, b_ada, w_in, ret_decay_f, ret_decay_b, w_ret_o, mla_q_norm, w_uq,
           mla_kv_norm, w_ukv, w_mla_o, w_out, ln1_g, ln1_b, w_gu, w_down, ln2_g, ln2_b):
    B, L, D = x.shape
    assert w_ada.shape[0] == DEPTH == 1
    i = 0

    c_rows = jnp.concatenate([c, c_ctx[None, :], jnp.zeros((8 - B - 1, D), F32)], axis=0)
    mod = _ada(c_rows, w_ada[i], b_ada[i]).reshape(8, 6, D)
    mod_l, mod_c = mod[:B], mod[B:B + 1]

    w_in_p, wuq_p, wukv_p = _prep_weights(w_in[i], w_uq[i], w_ukv[i])
    qn = mla_q_norm[i].reshape(1, -1)
    kvn = mla_kv_norm[i].reshape(1, -1)
    tables = _rope_tables(L)

    rq, rk, rv, srg, sgr, sgm, qm, km, vm = _in_proj(
        x, mod_l, w_in_p, qn, wuq_p, kvn, wukv_p, tables, is_ctx=False, tm=256)
    rk_c, rv_c, km_c, vm_c = _in_proj(
        ctx, mod_c, w_in_p, None, None, kvn, wukv_p, None, is_ctx=True, tm=256)

    dec = jnp.stack([jnp.broadcast_to(ret_decay_f[i][:, None], (RET_HEADS, LANES)),
                     jnp.broadcast_to(ret_decay_b[i][:, None], (RET_HEADS, LANES))], axis=1)
    o_ret = _retention(dec, rq, rk, rv, rk_c, rv_c)
    att = _attention(qm, km, vm, km_c, vm_c, tq=256)

    x1 = _merge(x, mod_l, o_ret, srg, att, sgr, sgm,
                w_ret_o[i].astype(BF16), w_mla_o[i].astype(BF16), w_out[i].astype(BF16),
                ln1_g[i].reshape(1, D), ln1_b[i].reshape(1, D), tm=256)
    return _ffn(x1, mod_l, w_gu[i].astype(BF16), w_down[i].astype(BF16),
                ln2_g[i].reshape(1, D), ln2_b[i].reshape(1, D), tm=256)
```

```python
import functools
import math

import numpy as np
import jax
import jax.numpy as jnp
from jax import lax
from jax.experimental import pallas as pl
from jax.experimental.pallas import tpu as pltpu

F32 = jnp.float32
BF16 = jnp.bfloat16

D_MODEL = 1024
GRID_W = 64
RET_HEADS = 4
RET_QK = 128
RET_V = 256
RET_CHUNK = 128
RET_ROPE_BASE = 10000.0
MLA_HEADS = 8
MLA_NOPE = 64
MLA_ROPE = 32
MLA_V = 64
MLA_Q_RANK = 384
MLA_KV_RANK = 256
ROPE_BASE = 10000.0
D_FF = -(-8 * D_MODEL // (3 * 256)) * 256
LN_EPS = 1e-5
RMS_EPS = 1e-6
DEPTH = 1
ALPHA = (2.0 * DEPTH) ** 0.25

LANES = 128
MLA_HEAD_PAD = LANES
VMEM_LIMIT = 56 * 1024 * 1024

_RQ = RET_HEADS * RET_QK
_RV = RET_HEADS * RET_V
_COLS = {}
_off = 0
for _name, _w in (("rq", _RQ), ("rv", _RV), ("rg", _RV), ("gr", D_MODEL), ("gm", D_MODEL),
                  ("dq", MLA_Q_RANK), ("dkv", MLA_KV_RANK), ("kr", LANES)):
    _COLS[_name] = (_off, _off + _w)
    _off += _w
IN_WIDTH_PAD = _off

_ROPE_X1 = 0
_NOPE_A = 16
_ROPE_X2 = 64
_NOPE_B = 80
_NOPE_SPLIT = 48
_RH = MLA_ROPE // 4


def _dot(a, b):
    return jnp.dot(a, b, preferred_element_type=F32)


def _dot_nt(a, b):
    return lax.dot_general(a, b, (((1,), (1,)), ((), ())), preferred_element_type=F32)


def _dot_tn(a, b):
    return lax.dot_general(a, b, (((0,), (0,)), ((), ())), preferred_element_type=F32)


def _sigmoid(x):
    return 1.0 / (1.0 + jnp.exp(-x))


def _silu(x):
    return x * _sigmoid(x)


def _layer_norm(x, g, b):
    mu = jnp.mean(x, axis=-1, keepdims=True)
    xc = x - mu
    var = jnp.mean(xc * xc, axis=-1, keepdims=True)
    return xc * lax.rsqrt(var + LN_EPS) * g + b


def _rope_half_roll(x, cos, sin_signed):
    return x * cos + pltpu.roll(x, LANES // 2, 1) * sin_signed


def _ada_kernel(c_ref, w_ref, b_ref, o_ref):
    c = c_ref[...]
    sc = _silu(c)
    o_ref[...] = jnp.dot(sc, w_ref[...], preferred_element_type=F32,
                         precision=lax.Precision.HIGHEST) + b_ref[...]


def _ada(c_rows, w_ada, b_ada):
    rows, d = c_rows.shape
    n = w_ada.shape[1]
    tn = 1536
    return pl.pallas_call(
        _ada_kernel,
        grid=(n // tn,),
        in_specs=[pl.BlockSpec((rows, d), lambda j: (0, 0)),
                  pl.BlockSpec((d, tn), lambda j: (0, j)),
                  pl.BlockSpec((1, tn), lambda j: (0, j))],
        out_specs=pl.BlockSpec((rows, tn), lambda j: (0, j)),
        out_shape=jax.ShapeDtypeStruct((rows, n), F32),
        compiler_params=pltpu.CompilerParams(dimension_semantics=("arbitrary",), vmem_limit_bytes=VMEM_LIMIT),
        name="ada",
    )(c_rows, w_ada, b_ada.reshape(1, n))


def _in_proj_kernel(*refs, is_ctx, q_scale):
    if is_ctx:
        (x_ref, mod_ref, w_ref, wrkt_ref, kvn_ref, wuk_ref, wuvt_ref,
         rkt_ref, rv_ref, km_ref, vmt_ref) = refs
    else:
        (x_ref, mod_ref, w_ref, wrkt_ref, qn_ref, wuq_ref, kvn_ref, wuk_ref, wuvt_ref,
         cr_ref, sr_ref, crt_ref, srt_ref, cm_ref, sm_ref,
         rq_ref, rkt_ref, rv_ref, srg_ref, sgr_ref, sgm_ref, qm_ref, km_ref, vmt_ref) = refs

    x = x_ref[0]
    sh1 = mod_ref[0, 0:1, :]
    s1 = mod_ref[0, 1:2, :]
    h = (x * (1.0 + s1) + sh1).astype(BF16)

    def proj(name):
        lo, hi = _COLS[name]
        return _dot(h, w_ref[:, lo:hi])

    k_scale = RET_QK ** -0.5
    rkt = _dot_nt(wrkt_ref[...], h)
    if is_ctx:
        rkt_ref[0] = (rkt * k_scale).astype(BF16)
    else:
        cr = cr_ref[...]
        sr = sr_ref[...]
        crt = crt_ref[...]
        srt = srt_ref[...]
        rq = proj("rq")
        half = RET_QK // 2
        for hh in range(RET_HEADS):
            sl = slice(hh * RET_QK, (hh + 1) * RET_QK)
            rq_ref[0, :, sl] = _rope_half_roll(rq[:, sl], cr, sr).astype(BF16)
            x1 = rkt[hh * RET_QK:hh * RET_QK + half]
            x2 = rkt[hh * RET_QK + half:(hh + 1) * RET_QK]
            rkt_ref[0, hh * RET_QK:hh * RET_QK + half, :] = ((x1 * crt - x2 * srt) * k_scale).astype(BF16)
            rkt_ref[0, hh * RET_QK + half:(hh + 1) * RET_QK, :] = ((x2 * crt + x1 * srt) * k_scale).astype(BF16)
        srg_ref[0] = _silu(proj("rg")).astype(BF16)
        sgr_ref[0] = _sigmoid(proj("gr")).astype(BF16)
        sgm_ref[0] = _sigmoid(proj("gm")).astype(BF16)
    rv_ref[0] = proj("rv").astype(BF16)

    def rms(v, g):
        return v * lax.rsqrt(jnp.mean(v * v, axis=-1, keepdims=True) + RMS_EPS) * g

    dkv = rms(proj("dkv"), kvn_ref[...]).astype(BF16)
    kv = _dot(dkv, wuk_ref[...])
    vmt_ref[0] = _dot_nt(wuvt_ref[...], dkv).astype(BF16)
    kr = proj("kr")
    if not is_ctx:
        cm = cm_ref[...]
        sm = sm_ref[...]
        kr = _rope_half_roll(kr, cm, sm)
        dq = rms(proj("dq"), qn_ref[...]).astype(BF16)
        q = _dot(dq, wuq_ref[...])
    for hh in range(MLA_HEADS):
        sl = slice(hh * MLA_HEAD_PAD, (hh + 1) * MLA_HEAD_PAD)
        km_ref[0, :, sl] = (kv[:, sl] + kr).astype(BF16)
        if not is_ctx:
            qm_ref[0, :, sl] = (_rope_half_roll(q[:, sl], cm, sm) * q_scale).astype(BF16)


def _in_proj(x, mod, w_in_p, wrkt, qn, wuq_p, kvn, wuk_p, wuvt, tables, *, is_ctx, tm):
    B, L, D = x.shape
    q_scale = (MLA_NOPE + MLA_ROPE) ** -0.5 * math.log2(math.e)
    kw = MLA_HEADS * MLA_HEAD_PAD
    vw = MLA_HEADS * MLA_V

    def const(shape):
        return pl.BlockSpec(shape, lambda b, i: (0,) * len(shape))

    def rows(width):
        return pl.BlockSpec((1, tm, width), lambda b, i: (b, i, 0))

    def out(width):
        return jax.ShapeDtypeStruct((B, L, width), BF16)

    x_spec = rows(D)
    def cols(height):
        return pl.BlockSpec((1, height, tm), lambda b, i: (b, 0, i))

    def out_t(height):
        return jax.ShapeDtypeStruct((B, height, L), BF16)

    if is_ctx:
        mod_spec = pl.BlockSpec((1, 6, D), lambda b, i: (0, 0, 0))
        in_specs = [x_spec, mod_spec, const(w_in_p.shape), const(wrkt.shape), const(kvn.shape),
                    const(wuk_p.shape), const(wuvt.shape)]
        args = (x, mod, w_in_p, wrkt, kvn, wuk_p, wuvt)
        out_specs = [cols(_RQ), rows(_RV), rows(kw), cols(vw)]
        out_shape = [out_t(_RQ), out(_RV), out(kw), out_t(vw)]
    else:
        mod_spec = pl.BlockSpec((1, 6, D), lambda b, i: (b, 0, 0))
        tab = pl.BlockSpec((tm, LANES), lambda b, i: (i, 0))
        tab_t = pl.BlockSpec((RET_QK // 2, tm), lambda b, i: (0, i))
        in_specs = [x_spec, mod_spec, const(w_in_p.shape), const(wrkt.shape), const(qn.shape), const(wuq_p.shape),
                    const(kvn.shape), const(wuk_p.shape), const(wuvt.shape), tab, tab, tab_t, tab_t, tab, tab]
        args = (x, mod, w_in_p, wrkt, qn, wuq_p, kvn, wuk_p, wuvt) + tuple(tables)
        out_specs = [rows(_RQ), cols(_RQ), rows(_RV), rows(_RV), rows(D), rows(D), rows(kw), rows(kw), cols(vw)]
        out_shape = [out(_RQ), out_t(_RQ), out(_RV), out(_RV), out(D), out(D), out(kw), out(kw), out_t(vw)]
    return pl.pallas_call(
        functools.partial(_in_proj_kernel, is_ctx=is_ctx, q_scale=q_scale),
        grid=(B, L // tm),
        in_specs=in_specs,
        out_specs=out_specs,
        out_shape=out_shape,
        compiler_params=pltpu.CompilerParams(dimension_semantics=("arbitrary", "arbitrary"),
                                             vmem_limit_bytes=VMEM_LIMIT),
        name="in_proj_ctx" if is_ctx else "in_proj",
    )(*args)


def _log_sigmoid(x):
    return jnp.minimum(x, 0.0) - jnp.log(1.0 + jnp.exp(-jnp.abs(x)))


RET_LOOKAHEAD = 2


def _retention_kernel(dec_ref, q_ref, kt_ref, v_ref, kct_ref, vc_ref, o_ref, u_ref, st_ref, *, n_lat, n_ctx):
    C = RET_CHUNK
    lgf = _log_sigmoid(dec_ref[0, 0:1, :])
    lgb = _log_sigmoid(dec_ref[0, 1:2, :])
    ii = lax.broadcasted_iota(jnp.int32, (C, C), 0).astype(F32)
    jj = lax.broadcasted_iota(jnp.int32, (C, C), 1).astype(F32)
    rel = ii - jj
    mask = jnp.where(rel > 0, jnp.exp(rel * lgf), jnp.where(rel < 0, jnp.exp(-rel * lgb), 2.0))
    pos_col = ii[:, 0:1]
    pos_row = jj[0:1, :]
    lgf1 = lgf[:, 0:1]
    lgb1 = lgb[:, 0:1]
    qdec_f = jnp.exp((pos_col + 1.0) * lgf1)
    qdec_b = jnp.exp((C - pos_col) * lgb1)
    kdec_f = jnp.exp((C - 1.0 - pos_row) * lgf)
    kdec_b = jnp.exp(pos_row * lgb)
    cdec_f = jnp.exp(C * lgf1)
    cdec_b = jnp.exp(C * lgb1)

    def increment(ktref, vref, n):
        kt = ktref[0, :, n * C:(n + 1) * C].astype(F32)
        lhs = jnp.concatenate([(kt * kdec_f).astype(BF16), (kt * kdec_b).astype(BF16)], axis=0)
        return _dot(lhs, vref[0, n * C:(n + 1) * C, :])

    zero = jnp.zeros((RET_QK, RET_V), F32)
    s_f, s_b = zero, zero
    u_ctx = [increment(kct_ref, vc_ref, n) for n in range(n_ctx)]
    for n in range(n_ctx):
        s_f = s_f * cdec_f + u_ctx[n][0:RET_QK]
        s_b = s_b * cdec_b + u_ctx[n_ctx - 1 - n][RET_QK:]

    for n in range(n_lat):
        u_ref[n] = increment(kt_ref, v_ref, n)

    for t in range(n_lat):
        nb = n_lat - 1 - t
        st_ref[t, 0:RET_QK, :] = s_f.astype(BF16)
        st_ref[nb, RET_QK:, :] = s_b.astype(BF16)
        s_f = s_f * cdec_f + u_ref[t, 0:RET_QK, :]
        s_b = s_b * cdec_b + u_ref[nb, RET_QK:, :]

    scores = {}

    def score(n):
        scores[n] = _dot(q_ref[0, n * C:(n + 1) * C, :], kt_ref[0, :, n * C:(n + 1) * C])

    def consume(n):
        rows = slice(n * C, (n + 1) * C)
        a = (scores.pop(n) * mask).astype(BF16)
        q = q_ref[0, rows, :].astype(F32)
        q_dec = jnp.concatenate([(q * qdec_f).astype(BF16), (q * qdec_b).astype(BF16)], axis=1)
        o = _dot(a, v_ref[0, rows, :]) + _dot(q_dec, st_ref[n])
        mu = jnp.mean(o, axis=-1, keepdims=True)
        oc = o - mu
        var = jnp.mean(oc * oc, axis=-1, keepdims=True)
        o_ref[0, rows, :] = (oc * lax.rsqrt(var + LN_EPS)).astype(BF16)

    for n in range(n_lat + RET_LOOKAHEAD):
        if n < n_lat:
            score(n)
        if n >= RET_LOOKAHEAD:
            consume(n - RET_LOOKAHEAD)


def _retention(dec, rq, rkt, rv, rkt_c, rv_c):
    B, L, _ = rq.shape
    Lc = rv_c.shape[1]
    n_lat, n_ctx = L // RET_CHUNK, Lc // RET_CHUNK
    return pl.pallas_call(
        functools.partial(_retention_kernel, n_lat=n_lat, n_ctx=n_ctx),
        grid=(B, RET_HEADS),
        in_specs=[pl.BlockSpec((1, 2, LANES), lambda b, h: (h, 0, 0)),
                  pl.BlockSpec((1, L, RET_QK), lambda b, h: (b, 0, h)),
                  pl.BlockSpec((1, RET_QK, L), lambda b, h: (b, h, 0)),
                  pl.BlockSpec((1, L, RET_V), lambda b, h: (b, 0, h)),
                  pl.BlockSpec((1, RET_QK, Lc), lambda b, h: (b, h, 0)),
                  pl.BlockSpec((1, Lc, RET_V), lambda b, h: (b, 0, h))],
        out_specs=pl.BlockSpec((1, L, RET_V), lambda b, h: (b, 0, h)),
        out_shape=jax.ShapeDtypeStruct((B, L, RET_HEADS * RET_V), BF16),
        scratch_shapes=[pltpu.VMEM((n_lat, 2 * RET_QK, RET_V), F32),
                        pltpu.VMEM((n_lat, 2 * RET_QK, RET_V), BF16)],
        compiler_params=pltpu.CompilerParams(dimension_semantics=("arbitrary", "arbitrary"),
                                             vmem_limit_bytes=VMEM_LIMIT),
        name="retention",
    )(dec, rq, rkt, rv, rkt_c, rv_c)


ATT_KEY_TILE = 512
ATT_ONES_ROWS = 16
ATT_MAX_ROWS = 64
ATT_LOOKAHEAD = 2


def _col_max(s):
    part = s[0:ATT_MAX_ROWS]
    for r in range(ATT_MAX_ROWS, s.shape[0], ATT_MAX_ROWS):
        part = jnp.maximum(part, s[r:r + ATT_MAX_ROWS])
    return jnp.max(part, axis=0, keepdims=True)


def _attention_kernel(q_ref, k_ref, vt_ref, kc_ref, vtc_ref, o_ref, *, l_lat, l_ctx):
    tiles = [(kc_ref, vtc_ref, lo, min(ATT_KEY_TILE, l_ctx - lo)) for lo in range(0, l_ctx, ATT_KEY_TILE)]
    tiles += [(k_ref, vt_ref, lo, ATT_KEY_TILE) for lo in range(0, l_lat, ATT_KEY_TILE)]
    heads = range(2)
    hs = [slice(hh * MLA_HEAD_PAD, (hh + 1) * MLA_HEAD_PAD) for hh in heads]
    vs = [slice(hh * MLA_V, (hh + 1) * MLA_V) for hh in heads]
    q = [q_ref[0, :, hs[hh]] for hh in heads]
    m = [None, None]
    acc = [None, None]
    items = [(tile, hh) for tile in tiles for hh in heads]
    scores = {}

    def score(i):
        (kref, _, lo, n), hh = items[i]
        scores[i] = _dot_nt(kref[0, lo:lo + n, hs[hh]], q[hh])

    def consume(i):
        (_, vref, lo, n), hh = items[i]
        s = scores.pop(i)
        t_max = _col_max(s)
        m_new = t_max if m[hh] is None else jnp.maximum(m[hh], t_max)
        p = jnp.exp2(s - m_new).astype(BF16)
        ones = jnp.ones((ATT_ONES_ROWS, n), BF16)
        part = _dot(jnp.concatenate([vref[0, vs[hh], lo:lo + n], ones], axis=0), p)
        acc[hh] = part if m[hh] is None else acc[hh] * jnp.exp2(m[hh] - m_new) + part
        m[hh] = m_new

    for i in range(len(items) + ATT_LOOKAHEAD):
        if i < len(items):
            score(i)
        if i >= ATT_LOOKAHEAD:
            consume(i - ATT_LOOKAHEAD)
    for hh in heads:
        o_ref[0, vs[hh], :] = (acc[hh][0:MLA_V] / acc[hh][MLA_V:MLA_V + 1]).astype(BF16)


def _attention(qm, km, vmt, km_c, vmt_c, *, tq):
    B, L, _ = qm.shape
    Lc = km_c.shape[1]
    assert L % ATT_KEY_TILE == 0 and Lc % LANES == 0
    pairs = MLA_HEADS // 2
    qk_w = 2 * MLA_HEAD_PAD
    v_w = 2 * MLA_V
    return pl.pallas_call(
        functools.partial(_attention_kernel, l_lat=L, l_ctx=Lc),
        grid=(B, pairs, L // tq),
        in_specs=[pl.BlockSpec((1, tq, qk_w), lambda b, p, i: (b, i, p)),
                  pl.BlockSpec((1, L, qk_w), lambda b, p, i: (b, 0, p)),
                  pl.BlockSpec((1, v_w, L), lambda b, p, i: (b, p, 0)),
                  pl.BlockSpec((1, Lc, qk_w), lambda b, p, i: (b, 0, p)),
                  pl.BlockSpec((1, v_w, Lc), lambda b, p, i: (b, p, 0))],
        out_specs=pl.BlockSpec((1, v_w, tq), lambda b, p, i: (b, p, i)),
        out_shape=jax.ShapeDtypeStruct((B, MLA_HEADS * MLA_V, L), BF16),
        compiler_params=pltpu.CompilerParams(dimension_semantics=("arbitrary", "arbitrary", "arbitrary"),
                                             vmem_limit_bytes=VMEM_LIMIT),
        name="attention",
    )(qm, km, vmt, km_c, vmt_c)


def _merge_kernel(x_ref, mod_ref, o_ret_ref, srg_ref, att_ref, sgr_ref, sgm_ref,
                  wro_ref, wmo_ref, wout_ref, g_ref, b_ref, y_ref):
    x = x_ref[0]
    g1 = mod_ref[0, 2:3, :]
    ret_in = (srg_ref[0].astype(F32) * o_ret_ref[0].astype(F32)).astype(BF16)
    ret = _dot(ret_in, wro_ref[...])
    mla = _dot_tn(att_ref[0], wmo_ref[...])
    mix =(sgr_ref[0].astype(F32) * ret + sgm_ref[0].astype(F32) * mla).astype(BF16)
    y = _dot(mix, wout_ref[...])
    y_ref[0] = _layer_norm(ALPHA * x + g1 * y, g_ref[...], b_ref[...])


def _merge(x, mod, o_ret, srg, att, sgr, sgm, wro, wmo, wout, ln_g, ln_b, *, tm):
    B, L, D = x.shape

    def const(shape):
        return pl.BlockSpec(shape, lambda b, i: (0,) * len(shape))

    def rows(width):
        return pl.BlockSpec((1, tm, width), lambda b, i: (b, i, 0))

    return pl.pallas_call(
        _merge_kernel,
        grid=(B, L // tm),
        in_specs=[rows(D), pl.BlockSpec((1, 6, D), lambda b, i: (b, 0, 0)),
                  rows(o_ret.shape[2]), rows(srg.shape[2]),
                  pl.BlockSpec((1, att.shape[1], tm), lambda b, i: (b, 0, i)), rows(D), rows(D),
                  const(wro.shape), const(wmo.shape), const(wout.shape), const(ln_g.shape), const(ln_b.shape)],
        out_specs=rows(D),
        out_shape=jax.ShapeDtypeStruct((B, L, D), F32),
        compiler_params=pltpu.CompilerParams(dimension_semantics=("arbitrary", "arbitrary"),
                                             vmem_limit_bytes=VMEM_LIMIT),
        name="merge",
    )(x, mod, o_ret, srg, att, sgr, sgm, wro, wmo, wout, ln_g, ln_b)


def _ffn_kernel(x_ref, mod_ref, wgu_ref, wdn_ref, g_ref, b_ref, y_ref):
    x = x_ref[0]
    sh2 = mod_ref[0, 3:4, :]
    s2 = mod_ref[0, 4:5, :]
    g2 = mod_ref[0, 5:6, :]
    h = (x * (1.0 + s2) + sh2).astype(BF16)
    a = _dot(h, wgu_ref[:, 0:D_FF])
    b = _dot(h, wgu_ref[:, D_FF:2 * D_FF])
    act = (_silu(a) * b).astype(BF16)
    f = _dot(act, wdn_ref[...])
    y_ref[0] = _layer_norm(ALPHA * x + g2 * f, g_ref[...], b_ref[...])


def _ffn(x, mod, wgu, wdn, ln_g, ln_b, *, tm):
    B, L, D = x.shape

    def const(shape):
        return pl.BlockSpec(shape, lambda b, i: (0,) * len(shape))

    rows = pl.BlockSpec((1, tm, D), lambda b, i: (b, i, 0))
    return pl.pallas_call(
        _ffn_kernel,
        grid=(B, L // tm),
        in_specs=[rows, pl.BlockSpec((1, 6, D), lambda b, i: (b, 0, 0)),
                  const(wgu.shape), const(wdn.shape), const(ln_g.shape), const(ln_b.shape)],
        out_specs=rows,
        out_shape=jax.ShapeDtypeStruct((B, L, D), F32),
        compiler_params=pltpu.CompilerParams(dimension_semantics=("arbitrary", "arbitrary"),
                                             vmem_limit_bytes=VMEM_LIMIT),
        name="ffn",
    )(x, mod, wgu, wdn, ln_g, ln_b)


def _rope_tables(L):
    t = np.arange(L, dtype=np.float64)
    half = RET_QK // 2
    fr = RET_ROPE_BASE ** (-np.arange(half, dtype=np.float64) / half)
    ang = t[:, None] * fr[None, :]
    cos_r = np.concatenate([np.cos(ang), np.cos(ang)], -1)
    sin_r = np.concatenate([-np.sin(ang), np.sin(ang)], -1)

    row = np.floor(t / GRID_W)
    col = t - row * GRID_W
    fm = ROPE_BASE ** (-np.arange(_RH, dtype=np.float64) / _RH)
    ang_rc = np.concatenate([row[:, None] * fm[None, :], col[:, None] * fm[None, :]], -1)
    cos_m = np.ones((L, LANES))
    sin_m = np.zeros((L, LANES))
    cos_m[:, _ROPE_X1:_ROPE_X1 + 2 * _RH] = np.cos(ang_rc)
    cos_m[:, _ROPE_X2:_ROPE_X2 + 2 * _RH] = np.cos(ang_rc)
    sin_m[:, _ROPE_X1:_ROPE_X1 + 2 * _RH] = -np.sin(ang_rc)
    sin_m[:, _ROPE_X2:_ROPE_X2 + 2 * _RH] = np.sin(ang_rc)
    cos_t = np.ascontiguousarray(np.cos(ang).T)
    sin_t = np.ascontiguousarray(np.sin(ang).T)
    return tuple(jnp.asarray(a, F32) for a in (cos_r, sin_r, cos_t, sin_t, cos_m, sin_m))


def _rope_cols_padded(w_rope):
    r = _RH
    x1 = jnp.concatenate([w_rope[:, 0:r], w_rope[:, 2 * r:3 * r]], axis=1)
    x2 = jnp.concatenate([w_rope[:, r:2 * r], w_rope[:, 3 * r:4 * r]], axis=1)
    return x1, x2


def _mla_head_cols(nope, rope):
    k = (nope if nope is not None else rope).shape[0]
    z = lambda n: jnp.zeros((k, n), F32)
    if rope is None:
        x1, x2 = z(2 * _RH), z(2 * _RH)
    else:
        x1, x2 = _rope_cols_padded(rope)
    if nope is None:
        na, nb = z(_NOPE_SPLIT), z(MLA_NOPE - _NOPE_SPLIT)
    else:
        na, nb = nope[:, :_NOPE_SPLIT], nope[:, _NOPE_SPLIT:]
    return jnp.concatenate([x1, na, x2, nb, z(LANES - _NOPE_B - (MLA_NOPE - _NOPE_SPLIT))], axis=1)


def _prep_weights(w_in, w_uq, w_ukv):
    c = 0
    parts = {}
    for name, w in (("rq", _RQ), ("rk", _RQ), ("rv", _RV), ("rg", _RV), ("dq", MLA_Q_RANK),
                    ("dkv", MLA_KV_RANK), ("kr", MLA_ROPE), ("gr", D_MODEL), ("gm", D_MODEL)):
        parts[name] = w_in[:, c:c + w]
        c += w
    parts["kr"] = _mla_head_cols(None, parts["kr"])
    w_in_p = jnp.concatenate([parts[n] for n in _COLS], axis=1).astype(BF16)
    wrkt = parts["rk"].T.astype(BF16)

    hq = MLA_NOPE + MLA_ROPE
    wuq_p = jnp.concatenate(
        [_mla_head_cols(w_uq[:, h * hq:h * hq + MLA_NOPE], w_uq[:, h * hq + MLA_NOPE:(h + 1) * hq])
         for h in range(MLA_HEADS)], axis=1).astype(BF16)
    hkv = MLA_NOPE + MLA_V
    wuk_p = jnp.concatenate([_mla_head_cols(w_ukv[:, h * hkv:h * hkv + MLA_NOPE], None)
                             for h in range(MLA_HEADS)], axis=1)
    wuv = jnp.concatenate([w_ukv[:, h * hkv + MLA_NOPE:(h + 1) * hkv] for h in range(MLA_HEADS)], axis=1)
    return w_in_p, wrkt, wuq_p, wuk_p.astype(BF16), wuv.T.astype(BF16)


def kernel(x, c, ctx, c_ctx, w_ada, b_ada, w_in, ret_decay_f, ret_decay_b, w_ret_o, mla_q_norm, w_uq,
           mla_kv_norm, w_ukv, w_mla_o, w_out, ln1_g, ln1_b, w_gu, w_down, ln2_g, ln2_b):
    B, L, D = x.shape
    assert w_ada.shape[0] == DEPTH == 1
    i = 0

    c_rows = jnp.concatenate([c, c_ctx[None, :], jnp.zeros((8 - B - 1, D), F32)], axis=0)
    mod = _ada(c_rows, w_ada[i], b_ada[i]).reshape(8, 6, D)
    mod_l, mod_c = mod[:B], mod[B:B + 1]

    w_in_p, wrkt, wuq_p, wuk_p, wuvt = _prep_weights(w_in[i], w_uq[i], w_ukv[i])
    qn = mla_q_norm[i].reshape(1, -1)
    kvn = mla_kv_norm[i].reshape(1, -1)
    tables = _rope_tables(L)

    rq, rk, rv, srg, sgr, sgm, qm, km, vm = _in_proj(
        x, mod_l, w_in_p, wrkt, qn, wuq_p, kvn, wuk_p, wuvt, tables, is_ctx=False, tm=256)
    rk_c, rv_c, km_c, vm_c = _in_proj(
        ctx, mod_c, w_in_p, wrkt, None, None, kvn, wuk_p, wuvt, None, is_ctx=True, tm=256)

    dec = jnp.stack([jnp.broadcast_to(ret_decay_f[i][:, None], (RET_HEADS, LANES)),
                     jnp.broadcast_to(ret_decay_b[i][:, None], (RET_HEADS, LANES))], axis=1)
    o_ret = _retention(dec, rq, rk, rv, rk_c, rv_c)
    att = _attention(qm, km, vm, km_c, vm_c, tq=256)

    x1 = _merge(x, mod_l, o_ret, srg, att, sgr, sgm,
                w_ret_o[i].astype(BF16), w_mla_o[i].astype(BF16), w_out[i].astype(BF16),
                ln1_g[i].reshape(1, D), ln1_b[i].reshape(1, D), tm=256)
    return _ffn(x1, mod_l, w_gu[i].astype(BF16), w_down[i].astype(BF16),
                ln2_g[i].reshape(1, D), ln2_b[i].reshape(1, D), tm=256)
```

```python
import functools
import math

import numpy as np
import jax
import jax.numpy as jnp
from jax import lax
from jax.experimental import pallas as pl
from jax.experimental.pallas import tpu as pltpu

F32 = jnp.float32
BF16 = jnp.bfloat16

D_MODEL = 1024
GRID_W = 64
RET_HEADS = 4
RET_QK = 128
RET_V = 256
RET_CHUNK = 128
RET_ROPE_BASE = 10000.0
MLA_HEADS = 8
MLA_NOPE = 64
MLA_ROPE = 32
MLA_V = 64
MLA_Q_RANK = 384
MLA_KV_RANK = 256
ROPE_BASE = 10000.0
D_FF = -(-8 * D_MODEL // (3 * 256)) * 256
LN_EPS = 1e-5
RMS_EPS = 1e-6
DEPTH = 1
ALPHA = (2.0 * DEPTH) ** 0.25

LANES = 128
MLA_HEAD_PAD = LANES
VMEM_LIMIT = 56 * 1024 * 1024

_RQ = RET_HEADS * RET_QK
_RV = RET_HEADS * RET_V
_COLS = {}
_off = 0
for _name, _w in (("rq", _RQ), ("rv", _RV), ("rg", _RV), ("gr", D_MODEL), ("gm", D_MODEL),
                  ("dq", MLA_Q_RANK), ("dkv", MLA_KV_RANK), ("kr", LANES)):
    _COLS[_name] = (_off, _off + _w)
    _off += _w
IN_WIDTH_PAD = _off

_ROPE_X1 = 0
_NOPE_A = 16
_ROPE_X2 = 64
_NOPE_B = 80
_NOPE_SPLIT = 48
_RH = MLA_ROPE // 4


def _dot(a, b):
    return jnp.dot(a, b, preferred_element_type=F32)


def _dot_nt(a, b):
    return lax.dot_general(a, b, (((1,), (1,)), ((), ())), preferred_element_type=F32)


def _dot_tn(a, b):
    return lax.dot_general(a, b, (((0,), (0,)), ((), ())), preferred_element_type=F32)


def _sigmoid(x):
    return 1.0 / (1.0 + jnp.exp(-x))


def _silu(x):
    return x * _sigmoid(x)


def _layer_norm(x, g, b):
    mu = jnp.mean(x, axis=-1, keepdims=True)
    xc = x - mu
    var = jnp.mean(xc * xc, axis=-1, keepdims=True)
    return xc * lax.rsqrt(var + LN_EPS) * g + b


def _rope_half_roll(x, cos, sin_signed):
    return x * cos + pltpu.roll(x, LANES // 2, 1) * sin_signed


def _ada_kernel(c_ref, w_ref, b_ref, o_ref):
    c = c_ref[...]
    sc = _silu(c)
    o_ref[...] = jnp.dot(sc, w_ref[...], preferred_element_type=F32,
                         precision=lax.Precision.HIGHEST) + b_ref[...]


def _ada(c_rows, w_ada, b_ada):
    rows, d = c_rows.shape
    n = w_ada.shape[1]
    tn = 1536
    return pl.pallas_call(
        _ada_kernel,
        grid=(n // tn,),
        in_specs=[pl.BlockSpec((rows, d), lambda j: (0, 0)),
                  pl.BlockSpec((d, tn), lambda j: (0, j)),
                  pl.BlockSpec((1, tn), lambda j: (0, j))],
        out_specs=pl.BlockSpec((rows, tn), lambda j: (0, j)),
        out_shape=jax.ShapeDtypeStruct((rows, n), F32),
        compiler_params=pltpu.CompilerParams(dimension_semantics=("arbitrary",), vmem_limit_bytes=VMEM_LIMIT),
        name="ada",
    )(c_rows, w_ada, b_ada.reshape(1, n))


def _in_proj_kernel(*refs, is_ctx, q_scale):
    if is_ctx:
        (x_ref, mod_ref, w_ref, wrkt_ref, kvn_ref, wuk_ref, wuvt_ref,
         rkt_ref, rv_ref, km_ref, vmt_ref) = refs
    else:
        (x_ref, mod_ref, w_ref, wrkt_ref, qn_ref, wuqt_ref, kvn_ref, wuk_ref, wuvt_ref,
         cr_ref, sr_ref, crt_ref, srt_ref, cm_ref, sm_ref, cmt_ref, smt_ref,
         rq_ref, rkt_ref, rv_ref, srg_ref, sgr_ref, sgm_ref, qmt_ref, km_ref, vmt_ref) = refs

    x = x_ref[0]
    sh1 = mod_ref[0, 0:1, :]
    s1 = mod_ref[0, 1:2, :]
    h = (x * (1.0 + s1) + sh1).astype(BF16)

    def proj(name):
        lo, hi = _COLS[name]
        return _dot(h, w_ref[:, lo:hi])

    def rms(v, g):
        return v * lax.rsqrt(jnp.mean(v * v, axis=-1, keepdims=True) + RMS_EPS) * g

    k_scale = RET_QK ** -0.5
    dkv_raw = proj("dkv")
    kr = proj("kr")
    if not is_ctx:
        dq_raw = proj("dq")
    rkt = _dot_nt(wrkt_ref[...], h)
    if not is_ctx:
        rq = proj("rq")

    dkv = rms(dkv_raw, kvn_ref[...]).astype(BF16)
    kv = _dot(dkv, wuk_ref[...])
    vmt_ref[0] = _dot_nt(wuvt_ref[...], dkv).astype(BF16)
    if not is_ctx:
        dq = rms(dq_raw, qn_ref[...]).astype(BF16)
        qt = _dot_nt(wuqt_ref[...], dq)
        srg_ref[0] = _silu(proj("rg")).astype(BF16)
        sgr_ref[0] = _sigmoid(proj("gr")).astype(BF16)
        sgm_ref[0] = _sigmoid(proj("gm")).astype(BF16)
    rv_ref[0] = proj("rv").astype(BF16)

    if is_ctx:
        rkt_ref[0] = (rkt * k_scale).astype(BF16)
    else:
        cr = cr_ref[...]
        sr = sr_ref[...]
        crt = crt_ref[...]
        srt = srt_ref[...]
        half = RET_QK // 2
        for hh in range(RET_HEADS):
            sl = slice(hh * RET_QK, (hh + 1) * RET_QK)
            rq_ref[0, :, sl] = _rope_half_roll(rq[:, sl], cr, sr).astype(BF16)
            x1 = rkt[hh * RET_QK:hh * RET_QK + half]
            x2 = rkt[hh * RET_QK + half:(hh + 1) * RET_QK]
            rkt_ref[0, hh * RET_QK:hh * RET_QK + half, :] = ((x1 * crt - x2 * srt) * k_scale).astype(BF16)
            rkt_ref[0, hh * RET_QK + half:(hh + 1) * RET_QK, :] = ((x2 * crt + x1 * srt) * k_scale).astype(BF16)
        cm = cm_ref[...]
        sm = sm_ref[...]
        kr = _rope_half_roll(kr, cm, sm)
        cmt = cmt_ref[...]
        smt = smt_ref[...]
        hp = MLA_HEAD_PAD // 2
    for hh in range(MLA_HEADS):
        sl = slice(hh * MLA_HEAD_PAD, (hh + 1) * MLA_HEAD_PAD)
        km_ref[0, :, sl] = (kv[:, sl] + kr).astype(BF16)
        if not is_ctx:
            x1 = qt[sl.start:sl.start + hp]
            x2 = qt[sl.start + hp:sl.stop]
            qmt_ref[0, sl.start:sl.start + hp, :] = ((x1 * cmt[:hp] + x2 * smt[:hp]) * q_scale).astype(BF16)
            qmt_ref[0, sl.start + hp:sl.stop, :] = ((x2 * cmt[hp:] + x1 * smt[hp:]) * q_scale).astype(BF16)


def _in_proj(x, mod, w_in_p, wrkt, qn, wuq_p, kvn, wuk_p, wuvt, tables, *, is_ctx, tm):
    B, L, D = x.shape
    q_scale = (MLA_NOPE + MLA_ROPE) ** -0.5 * math.log2(math.e)
    kw = MLA_HEADS * MLA_HEAD_PAD
    vw = MLA_HEADS * MLA_V

    def const(shape):
        return pl.BlockSpec(shape, lambda b, i: (0,) * len(shape), pipeline_mode=pl.Buffered(1))

    def rows(width):
        return pl.BlockSpec((1, tm, width), lambda b, i: (b, i, 0))

    def out(width):
        return jax.ShapeDtypeStruct((B, L, width), BF16)

    x_spec = rows(D)
    def cols(height):
        return pl.BlockSpec((1, height, tm), lambda b, i: (b, 0, i))

    def out_t(height):
        return jax.ShapeDtypeStruct((B, height, L), BF16)

    if is_ctx:
        mod_spec = pl.BlockSpec((1, 6, D), lambda b, i: (0, 0, 0))
        in_specs = [x_spec, mod_spec, const(w_in_p.shape), const(wrkt.shape), const(kvn.shape),
                    const(wuk_p.shape), const(wuvt.shape)]
        args = (x, mod, w_in_p, wrkt, kvn, wuk_p, wuvt)
        out_specs = [cols(_RQ), rows(_RV), rows(kw), cols(vw)]
        out_shape = [out_t(_RQ), out(_RV), out(kw), out_t(vw)]
    else:
        mod_spec = pl.BlockSpec((1, 6, D), lambda b, i: (b, 0, 0))
        tab = pl.BlockSpec((tm, LANES), lambda b, i: (i, 0))
        tab_t = pl.BlockSpec((RET_QK // 2, tm), lambda b, i: (0, i))
        tab_mt = pl.BlockSpec((MLA_HEAD_PAD, tm), lambda b, i: (0, i))
        in_specs = [x_spec, mod_spec, const(w_in_p.shape), const(wrkt.shape), const(qn.shape), const(wuq_p.shape),
                    const(kvn.shape), const(wuk_p.shape), const(wuvt.shape),
                    tab, tab, tab_t, tab_t, tab, tab, tab_mt, tab_mt]
        args = (x, mod, w_in_p, wrkt, qn, wuq_p, kvn, wuk_p, wuvt) + tuple(tables)
        out_specs = [rows(_RQ), cols(_RQ), rows(_RV), rows(_RV), rows(D), rows(D), cols(kw), rows(kw), cols(vw)]
        out_shape = [out(_RQ), out_t(_RQ), out(_RV), out(_RV), out(D), out(D), out_t(kw), out(kw), out_t(vw)]
    return pl.pallas_call(
        functools.partial(_in_proj_kernel, is_ctx=is_ctx, q_scale=q_scale),
        grid=(B, L // tm),
        in_specs=in_specs,
        out_specs=out_specs,
        out_shape=out_shape,
        compiler_params=pltpu.CompilerParams(dimension_semantics=("arbitrary", "arbitrary"),
                                             vmem_limit_bytes=VMEM_LIMIT),
        name="in_proj_ctx" if is_ctx else "in_proj",
    )(*args)


def _log_sigmoid(x):
    return jnp.minimum(x, 0.0) - jnp.log(1.0 + jnp.exp(-jnp.abs(x)))


RET_LOOKAHEAD = 2


def _retention_kernel(dec_ref, q_ref, kt_ref, v_ref, kct_ref, vc_ref, o_ref, u_ref, st_ref, *, n_lat, n_ctx):
    C = RET_CHUNK
    lgf = _log_sigmoid(dec_ref[0, 0:1, :])
    lgb = _log_sigmoid(dec_ref[0, 1:2, :])
    ii = lax.broadcasted_iota(jnp.int32, (C, C), 0).astype(F32)
    jj = lax.broadcasted_iota(jnp.int32, (C, C), 1).astype(F32)
    rel = ii - jj
    mask = jnp.where(rel > 0, jnp.exp(rel * lgf), jnp.where(rel < 0, jnp.exp(-rel * lgb), 2.0))
    pos_col = ii[:, 0:1]
    pos_row = jj[0:1, :]
    lgf1 = lgf[:, 0:1]
    lgb1 = lgb[:, 0:1]
    qdec_f = jnp.exp((pos_col + 1.0) * lgf1)
    qdec_b = jnp.exp((C - pos_col) * lgb1)
    kdec_f = jnp.exp((C - 1.0 - pos_row) * lgf)
    kdec_b = jnp.exp(pos_row * lgb)
    cdec_f = jnp.exp(C * lgf1)
    cdec_b = jnp.exp(C * lgb1)

    def increment(ktref, vref, n):
        kt = ktref[0, :, n * C:(n + 1) * C].astype(F32)
        lhs = jnp.concatenate([(kt * kdec_f).astype(BF16), (kt * kdec_b).astype(BF16)], axis=0)
        return _dot(lhs, vref[0, n * C:(n + 1) * C, :])

    zero = jnp.zeros((RET_QK, RET_V), F32)
    s_f, s_b = zero, zero
    u_ctx = [increment(kct_ref, vc_ref, n) for n in range(n_ctx)]
    for n in range(n_ctx):
        s_f = s_f * cdec_f + u_ctx[n][0:RET_QK]
        s_b = s_b * cdec_b + u_ctx[n_ctx - 1 - n][RET_QK:]

    for n in range(n_lat):
        u_ref[n] = increment(kt_ref, v_ref, n)

    for t in range(n_lat):
        nb = n_lat - 1 - t
        st_ref[t, 0:RET_QK, :] = s_f.astype(BF16)
        st_ref[nb, RET_QK:, :] = s_b.astype(BF16)
        s_f = s_f * cdec_f + u_ref[t, 0:RET_QK, :]
        s_b = s_b * cdec_b + u_ref[nb, RET_QK:, :]

    scores = {}

    def score(n):
        scores[n] = _dot(q_ref[0, n * C:(n + 1) * C, :], kt_ref[0, :, n * C:(n + 1) * C])

    def consume(n):
        rows = slice(n * C, (n + 1) * C)
        a = (scores.pop(n) * mask).astype(BF16)
        q = q_ref[0, rows, :].astype(F32)
        q_dec = jnp.concatenate([(q * qdec_f).astype(BF16), (q * qdec_b).astype(BF16)], axis=1)
        o = _dot(a, v_ref[0, rows, :]) + _dot(q_dec, st_ref[n])
        mu = jnp.mean(o, axis=-1, keepdims=True)
        oc = o - mu
        var = jnp.mean(oc * oc, axis=-1, keepdims=True)
        o_ref[0, rows, :] = (oc * lax.rsqrt(var + LN_EPS)).astype(BF16)

    for n in range(n_lat + RET_LOOKAHEAD):
        if n < n_lat:
            score(n)
        if n >= RET_LOOKAHEAD:
            consume(n - RET_LOOKAHEAD)


def _retention(dec, rq, rkt, rv, rkt_c, rv_c):
    B, L, _ = rq.shape
    Lc = rv_c.shape[1]
    n_lat, n_ctx = L // RET_CHUNK, Lc // RET_CHUNK
    return pl.pallas_call(
        functools.partial(_retention_kernel, n_lat=n_lat, n_ctx=n_ctx),
        grid=(B, RET_HEADS),
        in_specs=[pl.BlockSpec((1, 2, LANES), lambda b, h: (h, 0, 0)),
                  pl.BlockSpec((1, L, RET_QK), lambda b, h: (b, 0, h)),
                  pl.BlockSpec((1, RET_QK, L), lambda b, h: (b, h, 0)),
                  pl.BlockSpec((1, L, RET_V), lambda b, h: (b, 0, h)),
                  pl.BlockSpec((1, RET_QK, Lc), lambda b, h: (b, h, 0)),
                  pl.BlockSpec((1, Lc, RET_V), lambda b, h: (b, 0, h))],
        out_specs=pl.BlockSpec((1, L, RET_V), lambda b, h: (b, 0, h)),
        out_shape=jax.ShapeDtypeStruct((B, L, RET_HEADS * RET_V), BF16),
        scratch_shapes=[pltpu.VMEM((n_lat, 2 * RET_QK, RET_V), F32),
                        pltpu.VMEM((n_lat, 2 * RET_QK, RET_V), BF16)],
        compiler_params=pltpu.CompilerParams(dimension_semantics=("arbitrary", "arbitrary"),
                                             vmem_limit_bytes=VMEM_LIMIT),
        name="retention",
    )(dec, rq, rkt, rv, rkt_c, rv_c)


ATT_SCORE_KEYS = 512
ATT_KEY_TILE = 512
ATT_Q_COLS = 256
ATT_ONES_ROWS = 16
ATT_MAX_ROWS = 64
ATT_LOOKAHEAD = 3


def _col_max(s):
    part = s[0:ATT_MAX_ROWS]
    for r in range(ATT_MAX_ROWS, s.shape[0], ATT_MAX_ROWS):
        part = jnp.maximum(part, s[r:r + ATT_MAX_ROWS])
    return jnp.max(part, axis=0, keepdims=True)


def _attention_kernel(q_ref, k_ref, vt_ref, kc_ref, vtc_ref, o_ref, *, l_lat, l_ctx):
    groups = [(kc_ref, vtc_ref, lo, min(ATT_SCORE_KEYS, l_ctx - lo)) for lo in range(0, l_ctx, ATT_SCORE_KEYS)]
    groups += [(k_ref, vt_ref, lo, ATT_SCORE_KEYS) for lo in range(0, l_lat, ATT_SCORE_KEYS)]
    tq = q_ref.shape[2]
    streams = [(hh, qb) for hh in range(2) for qb in range(tq // ATT_Q_COLS)]
    hs = [slice(hh * MLA_HEAD_PAD, (hh + 1) * MLA_HEAD_PAD) for hh in range(2)]
    vs = [slice(hh * MLA_V, (hh + 1) * MLA_V) for hh in range(2)]
    qs = [slice(qb * ATT_Q_COLS, (qb + 1) * ATT_Q_COLS) for qb in range(tq // ATT_Q_COLS)]
    m = {st: None for st in streams}
    acc = {st: None for st in streams}
    items = [(g, st, off, min(ATT_KEY_TILE, grp[3] - off))
             for g, grp in enumerate(groups) for st in streams for off in range(0, grp[3], ATT_KEY_TILE)]
    scores = {}

    def score(g, st):
        kref, _, lo, n = groups[g]
        hh, qb = st
        scores[g, st] = _dot(kref[0, lo:lo + n, hs[hh]], q_ref[0, hs[hh], qs[qb]])

    def consume(g, st, off, n):
        _, vref, lo, _ = groups[g]
        hh, _ = st
        s = scores[g, st][off:off + n]
        t_max = _col_max(s)
        m_new = t_max if m[st] is None else jnp.maximum(m[st], t_max)
        p = jnp.exp2(s - m_new).astype(BF16)
        ones = jnp.ones((ATT_ONES_ROWS, n), BF16)
        part = _dot(jnp.concatenate([vref[0, vs[hh], lo + off:lo + off + n], ones], axis=0), p)
        acc[st] = part if m[st] is None else acc[st] * jnp.exp2(m[st] - m_new) + part
        m[st] = m_new

    for i in range(len(items) + ATT_LOOKAHEAD):
        if i < len(items) and items[i][2] == 0:
            score(items[i][0], items[i][1])
        if i >= ATT_LOOKAHEAD:
            consume(*items[i - ATT_LOOKAHEAD])
    for hh, qb in streams:
        a = acc[hh, qb]
        o_ref[0, vs[hh], qs[qb]] = (a[0:MLA_V] / a[MLA_V:MLA_V + 1]).astype(BF16)


def _attention(qmt, km, vmt, km_c, vmt_c, *, tq):
    B, L, _ = km.shape
    Lc = km_c.shape[1]
    assert L % ATT_SCORE_KEYS == 0 and ATT_SCORE_KEYS % ATT_KEY_TILE == 0 and Lc % LANES == 0
    pairs = MLA_HEADS // 2
    qk_w = 2 * MLA_HEAD_PAD
    v_w = 2 * MLA_V
    return pl.pallas_call(
        functools.partial(_attention_kernel, l_lat=L, l_ctx=Lc),
        grid=(B, pairs, L // tq),
        in_specs=[pl.BlockSpec((1, qk_w, tq), lambda b, p, i: (b, p, i)),
                  pl.BlockSpec((1, L, qk_w), lambda b, p, i: (b, 0, p)),
                  pl.BlockSpec((1, v_w, L), lambda b, p, i: (b, p, 0)),
                  pl.BlockSpec((1, Lc, qk_w), lambda b, p, i: (b, 0, p)),
                  pl.BlockSpec((1, v_w, Lc), lambda b, p, i: (b, p, 0))],
        out_specs=pl.BlockSpec((1, v_w, tq), lambda b, p, i: (b, p, i)),
        out_shape=jax.ShapeDtypeStruct((B, MLA_HEADS * MLA_V, L), BF16),
        compiler_params=pltpu.CompilerParams(dimension_semantics=("arbitrary", "arbitrary", "arbitrary"),
                                             vmem_limit_bytes=VMEM_LIMIT),
        name="attention",
    )(qmt, km, vmt, km_c, vmt_c)


ROW_SUBTILE = 256


def _row_subtiles(rows):
    return [slice(lo, lo + ROW_SUBTILE) for lo in range(0, rows, ROW_SUBTILE)]


def _merge_kernel(x_ref, mod_ref, o_ret_ref, srg_ref, att_ref, sgr_ref, sgm_ref,
                  wro_ref, wmo_ref, wout_ref, g_ref, b_ref, y_ref):
    g1 = mod_ref[0, 2:3, :]
    subs = _row_subtiles(x_ref.shape[1])
    branch = []
    for r in subs:
        ret_in = (srg_ref[0, r, :].astype(F32) * o_ret_ref[0, r, :].astype(F32)).astype(BF16)
        ret = _dot(ret_in, wro_ref[...])
        mla = _dot_tn(att_ref[0, :, r], wmo_ref[...])
        branch.append((ret, mla))
    ys = []
    for r, (ret, mla) in zip(subs, branch):
        mix = (sgr_ref[0, r, :].astype(F32) * ret + sgm_ref[0, r, :].astype(F32) * mla).astype(BF16)
        ys.append(_dot(mix, wout_ref[...]))
    for r, y in zip(subs, ys):
        y_ref[0, r, :] = _layer_norm(ALPHA * x_ref[0, r, :] + g1 * y, g_ref[...], b_ref[...])


def _merge(x, mod, o_ret, srg, att, sgr, sgm, wro, wmo, wout, ln_g, ln_b, *, tm):
    B, L, D = x.shape

    def const(shape):
        return pl.BlockSpec(shape, lambda b, i: (0,) * len(shape), pipeline_mode=pl.Buffered(1))

    def rows(width):
        return pl.BlockSpec((1, tm, width), lambda b, i: (b, i, 0))

    return pl.pallas_call(
        _merge_kernel,
        grid=(B, L // tm),
        in_specs=[rows(D), pl.BlockSpec((1, 6, D), lambda b, i: (b, 0, 0)),
                  rows(o_ret.shape[2]), rows(srg.shape[2]),
                  pl.BlockSpec((1, att.shape[1], tm), lambda b, i: (b, 0, i)), rows(D), rows(D),
                  const(wro.shape), const(wmo.shape), const(wout.shape), const(ln_g.shape), const(ln_b.shape)],
        out_specs=rows(D),
        out_shape=jax.ShapeDtypeStruct((B, L, D), F32),
        compiler_params=pltpu.CompilerParams(dimension_semantics=("arbitrary", "arbitrary"),
                                             vmem_limit_bytes=VMEM_LIMIT),
        name="merge",
    )(x, mod, o_ret, srg, att, sgr, sgm, wro, wmo, wout, ln_g, ln_b)


def _ffn_kernel(x_ref, mod_ref, wgu_ref, wdn_ref, g_ref, b_ref, y_ref):
    sh2 = mod_ref[0, 3:4, :]
    s2 = mod_ref[0, 4:5, :]
    g2 = mod_ref[0, 5:6, :]
    subs = _row_subtiles(x_ref.shape[1])
    gate_up = []
    for r in subs:
        h = (x_ref[0, r, :] * (1.0 + s2) + sh2).astype(BF16)
        gate_up.append((_dot(h, wgu_ref[:, 0:D_FF]), _dot(h, wgu_ref[:, D_FF:2 * D_FF])))
    fs = [_dot((_silu(a) * b).astype(BF16), wdn_ref[...]) for a, b in gate_up]
    for r, f in zip(subs, fs):
        y_ref[0, r, :] = _layer_norm(ALPHA * x_ref[0, r, :] + g2 * f, g_ref[...], b_ref[...])


def _ffn(x, mod, wgu, wdn, ln_g, ln_b, *, tm):
    B, L, D = x.shape

    def const(shape):
        return pl.BlockSpec(shape, lambda b, i: (0,) * len(shape), pipeline_mode=pl.Buffered(1))

    rows = pl.BlockSpec((1, tm, D), lambda b, i: (b, i, 0))
    return pl.pallas_call(
        _ffn_kernel,
        grid=(B, L // tm),
        in_specs=[rows, pl.BlockSpec((1, 6, D), lambda b, i: (b, 0, 0)),
                  const(wgu.shape), const(wdn.shape), const(ln_g.shape), const(ln_b.shape)],
        out_specs=rows,
        out_shape=jax.ShapeDtypeStruct((B, L, D), F32),
        compiler_params=pltpu.CompilerParams(dimension_semantics=("arbitrary", "arbitrary"),
                                             vmem_limit_bytes=VMEM_LIMIT),
        name="ffn",
    )(x, mod, wgu, wdn, ln_g, ln_b)


def _rope_tables(L):
    t = np.arange(L, dtype=np.float64)
    half = RET_QK // 2
    fr = RET_ROPE_BASE ** (-np.arange(half, dtype=np.float64) / half)
    ang = t[:, None] * fr[None, :]
    cos_r = np.concatenate([np.cos(ang), np.cos(ang)], -1)
    sin_r = np.concatenate([-np.sin(ang), np.sin(ang)], -1)

    row = np.floor(t / GRID_W)
    col = t - row * GRID_W
    fm = ROPE_BASE ** (-np.arange(_RH, dtype=np.float64) / _RH)
    ang_rc = np.concatenate([row[:, None] * fm[None, :], col[:, None] * fm[None, :]], -1)
    cos_m = np.ones((L, LANES))
    sin_m = np.zeros((L, LANES))
    cos_m[:, _ROPE_X1:_ROPE_X1 + 2 * _RH] = np.cos(ang_rc)
    cos_m[:, _ROPE_X2:_ROPE_X2 + 2 * _RH] = np.cos(ang_rc)
    sin_m[:, _ROPE_X1:_ROPE_X1 + 2 * _RH] = -np.sin(ang_rc)
    sin_m[:, _ROPE_X2:_ROPE_X2 + 2 * _RH] = np.sin(ang_rc)
    cos_t = np.ascontiguousarray(np.cos(ang).T)
    sin_t = np.ascontiguousarray(np.sin(ang).T)
    cos_mt = np.ascontiguousarray(cos_m.T)
    sin_mt = np.ascontiguousarray(sin_m.T)
    return tuple(jnp.asarray(a, F32) for a in (cos_r, sin_r, cos_t, sin_t, cos_m, sin_m, cos_mt, sin_mt))


def _rope_cols_padded(w_rope):
    r = _RH
    x1 = jnp.concatenate([w_rope[:, 0:r], w_rope[:, 2 * r:3 * r]], axis=1)
    x2 = jnp.concatenate([w_rope[:, r:2 * r], w_rope[:, 3 * r:4 * r]], axis=1)
    return x1, x2


def _mla_head_cols(nope, rope):
    k = (nope if nope is not None else rope).shape[0]
    z = lambda n: jnp.zeros((k, n), F32)
    if rope is None:
        x1, x2 = z(2 * _RH), z(2 * _RH)
    else:
        x1, x2 = _rope_cols_padded(rope)
    if nope is None:
        na, nb = z(_NOPE_SPLIT), z(MLA_NOPE - _NOPE_SPLIT)
    else:
        na, nb = nope[:, :_NOPE_SPLIT], nope[:, _NOPE_SPLIT:]
    return jnp.concatenate([x1, na, x2, nb, z(LANES - _NOPE_B - (MLA_NOPE - _NOPE_SPLIT))], axis=1)


def _prep_weights(w_in, w_uq, w_ukv):
    c = 0
    parts = {}
    for name, w in (("rq", _RQ), ("rk", _RQ), ("rv", _RV), ("rg", _RV), ("dq", MLA_Q_RANK),
                    ("dkv", MLA_KV_RANK), ("kr", MLA_ROPE), ("gr", D_MODEL), ("gm", D_MODEL)):
        parts[name] = w_in[:, c:c + w]
        c += w
    parts["kr"] = _mla_head_cols(None, parts["kr"])
    w_in_p = jnp.concatenate([parts[n] for n in _COLS], axis=1).astype(BF16)
    wrkt = parts["rk"].T.astype(BF16)

    hq = MLA_NOPE + MLA_ROPE
    wuq_p = jnp.concatenate(
        [_mla_head_cols(w_uq[:, h * hq:h * hq + MLA_NOPE], w_uq[:, h * hq + MLA_NOPE:(h + 1) * hq])
         for h in range(MLA_HEADS)], axis=1).T.astype(BF16)
    hkv = MLA_NOPE + MLA_V
    wuk_p = jnp.concatenate([_mla_head_cols(w_ukv[:, h * hkv:h * hkv + MLA_NOPE], None)
                             for h in range(MLA_HEADS)], axis=1)
    wuv = jnp.concatenate([w_ukv[:, h * hkv + MLA_NOPE:(h + 1) * hkv] for h in range(MLA_HEADS)], axis=1)
    return w_in_p, wrkt, wuq_p, wuk_p.astype(BF16), wuv.T.astype(BF16)


def kernel(x, c, ctx, c_ctx, w_ada, b_ada, w_in, ret_decay_f, ret_decay_b, w_ret_o, mla_q_norm, w_uq,
           mla_kv_norm, w_ukv, w_mla_o, w_out, ln1_g, ln1_b, w_gu, w_down, ln2_g, ln2_b):
    B, L, D = x.shape
    assert w_ada.shape[0] == DEPTH == 1
    i = 0

    c_rows = jnp.concatenate([c, c_ctx[None, :], jnp.zeros((8 - B - 1, D), F32)], axis=0)
    mod = _ada(c_rows, w_ada[i], b_ada[i]).reshape(8, 6, D)
    mod_l, mod_c = mod[:B], mod[B:B + 1]

    w_in_p, wrkt, wuq_p, wuk_p, wuvt = _prep_weights(w_in[i], w_uq[i], w_ukv[i])
    qn = mla_q_norm[i].reshape(1, -1)
    kvn = mla_kv_norm[i].reshape(1, -1)
    tables = _rope_tables(L)

    rq, rk, rv, srg, sgr, sgm, qm, km, vm = _in_proj(
        x, mod_l, w_in_p, wrkt, qn, wuq_p, kvn, wuk_p, wuvt, tables, is_ctx=False, tm=256)
    rk_c, rv_c, km_c, vm_c = _in_proj(
        ctx, mod_c, w_in_p, wrkt, None, None, kvn, wuk_p, wuvt, None, is_ctx=True, tm=256)

    dec = jnp.stack([jnp.broadcast_to(ret_decay_f[i][:, None], (RET_HEADS, LANES)),
                     jnp.broadcast_to(ret_decay_b[i][:, None], (RET_HEADS, LANES))], axis=1)
    o_ret = _retention(dec, rq, rk, rv, rk_c, rv_c)
    att = _attention(qm, km, vm, km_c, vm_c, tq=512)

    x1 = _merge(x, mod_l, o_ret, srg, att, sgr, sgm,
                w_ret_o[i].astype(BF16), w_mla_o[i].astype(BF16), w_out[i].astype(BF16),
                ln1_g[i].reshape(1, D), ln1_b[i].reshape(1, D), tm=512)
    return _ffn(x1, mod_l, w_gu[i].astype(BF16), w_down[i].astype(BF16),
                ln2_g[i].reshape(1, D), ln2_b[i].reshape(1, D), tm=512)
```

```python
import functools
import math

import numpy as np
import jax
import jax.numpy as jnp
from jax import lax
from jax.experimental import pallas as pl
from jax.experimental.pallas import tpu as pltpu

F32 = jnp.float32
BF16 = jnp.bfloat16

D_MODEL = 1024
GRID_W = 64
RET_HEADS = 4
RET_QK = 128
RET_V = 256
RET_CHUNK = 128
RET_ROPE_BASE = 10000.0
MLA_HEADS = 8
MLA_NOPE = 64
MLA_ROPE = 32
MLA_V = 64
MLA_Q_RANK = 384
MLA_KV_RANK = 256
ROPE_BASE = 10000.0
D_FF = -(-8 * D_MODEL // (3 * 256)) * 256
LN_EPS = 1e-5
RMS_EPS = 1e-6
DEPTH = 1
ALPHA = (2.0 * DEPTH) ** 0.25

LANES = 128
MLA_HEAD_PAD = LANES
VMEM_LIMIT = 56 * 1024 * 1024

_RQ = RET_HEADS * RET_QK
_RV = RET_HEADS * RET_V
_KR_OFF = 2 * _RQ + 2 * _RV + MLA_Q_RANK + MLA_KV_RANK
_COLS = {
    "rq": (0, 0, _RQ),
    "rk": (0, _RQ, 2 * _RQ),
    "rv": (0, 2 * _RQ, 2 * _RQ + _RV),
    "rg": (0, 2 * _RQ + _RV, 2 * _RQ + 2 * _RV),
    "dq": (0, 2 * _RQ + 2 * _RV, 2 * _RQ + 2 * _RV + MLA_Q_RANK),
    "dkv": (0, 2 * _RQ + 2 * _RV + MLA_Q_RANK, _KR_OFF),
    "kr": (1, 0, LANES),
    "gr": (2, 0, D_MODEL),
    "gm": (2, D_MODEL, 2 * D_MODEL),
}

_ROPE_X1 = 0
_NOPE_A = 16
_ROPE_X2 = 64
_NOPE_B = 80
_NOPE_SPLIT = 48
_RH = MLA_ROPE // 4


def _dot(a, b):
    return jnp.dot(a, b, preferred_element_type=F32)


def _dot_nt(a, b):
    return lax.dot_general(a, b, (((1,), (1,)), ((), ())), preferred_element_type=F32)


def _dot_tn(a, b):
    return lax.dot_general(a, b, (((0,), (0,)), ((), ())), preferred_element_type=F32)


def _sigmoid(x):
    return 1.0 / (1.0 + jnp.exp(-x))


def _silu(x):
    return x * _sigmoid(x)


def _layer_norm(x, g, b):
    mu = jnp.mean(x, axis=-1, keepdims=True)
    xc = x - mu
    var = jnp.mean(xc * xc, axis=-1, keepdims=True)
    return xc * lax.rsqrt(var + LN_EPS) * g + b


def _rope_half_roll(x, cos, sin_signed):
    return x * cos + pltpu.roll(x, LANES // 2, 1) * sin_signed


def _ada_kernel(c_ref, w_ref, b_ref, o_ref):
    sc = _silu(c_ref[...]).astype(BF16)
    o_ref[...] = _dot(sc, w_ref[...].astype(BF16)) + b_ref[...]


def _ada(c_rows, w_ada, b_ada):
    rows, d = c_rows.shape
    n = w_ada.shape[1]
    tn = 1536
    return pl.pallas_call(
        _ada_kernel,
        grid=(n // tn,),
        in_specs=[pl.BlockSpec((rows, d), lambda j: (0, 0)),
                  pl.BlockSpec((d, tn), lambda j: (0, j)),
                  pl.BlockSpec((1, tn), lambda j: (0, j))],
        out_specs=pl.BlockSpec((rows, tn), lambda j: (0, j)),
        out_shape=jax.ShapeDtypeStruct((rows, n), F32),
        compiler_params=pltpu.CompilerParams(dimension_semantics=("arbitrary",), vmem_limit_bytes=VMEM_LIMIT),
        name="ada",
    )(c_rows, w_ada, b_ada.reshape(1, n))


def _in_proj_kernel(*refs, is_ctx, q_scale):
    if is_ctx:
        (x_ref, mod_ref, wa_ref, wkr_ref, wg_ref, wrkt_ref, kvn_ref, wuk_ref, wuvt_ref,
         rkt_ref, rv_ref, km_ref, vmt_ref) = refs
    else:
        (x_ref, mod_ref, wa_ref, wkr_ref, wg_ref, wrkt_ref, qn_ref, wuqt_ref, kvn_ref, wuk_ref, wuvt_ref,
         cr_ref, sr_ref, crt_ref, srt_ref, cm_ref, sm_ref, cmt_ref, smt_ref,
         rq_ref, rkt_ref, rv_ref, srg_ref, sgr_ref, sgm_ref, qmt_ref, km_ref, vmt_ref) = refs

    x = x_ref[0]
    sh1 = mod_ref[0, 0:1, :]
    s1 = mod_ref[0, 1:2, :]
    h = (x * (1.0 + s1) + sh1).astype(BF16)

    def proj(name):
        group, lo, hi = _COLS[name]
        return _dot(h, (wa_ref, wkr_ref, wg_ref)[group][:, lo:hi])

    def rms(v, g):
        return v * lax.rsqrt(jnp.mean(v * v, axis=-1, keepdims=True) + RMS_EPS) * g

    k_scale = RET_QK ** -0.5
    dkv_raw = proj("dkv")
    kr = proj("kr")
    if not is_ctx:
        dq_raw = proj("dq")
    rkt = _dot_nt(wrkt_ref[...], h)
    if not is_ctx:
        rq = proj("rq")

    dkv = rms(dkv_raw, kvn_ref[...]).astype(BF16)
    kv = _dot(dkv, wuk_ref[...])
    vmt_ref[0] = _dot_nt(wuvt_ref[...], dkv).astype(BF16)
    if not is_ctx:
        dq = rms(dq_raw, qn_ref[...]).astype(BF16)
        qt = _dot_nt(wuqt_ref[...], dq)
        srg_ref[0] = _silu(proj("rg")).astype(BF16)
        sgr_ref[0] = _sigmoid(proj("gr")).astype(BF16)
        sgm_ref[0] = _sigmoid(proj("gm")).astype(BF16)
    rv_ref[0] = proj("rv").astype(BF16)

    if is_ctx:
        rkt_ref[0] = (rkt * k_scale).astype(BF16)
    else:
        cr = cr_ref[...]
        sr = sr_ref[...]
        crt = crt_ref[...]
        srt = srt_ref[...]
        half = RET_QK // 2
        for hh in range(RET_HEADS):
            sl = slice(hh * RET_QK, (hh + 1) * RET_QK)
            rq_ref[0, :, sl] = _rope_half_roll(rq[:, sl], cr, sr).astype(BF16)
            x1 = rkt[hh * RET_QK:hh * RET_QK + half]
            x2 = rkt[hh * RET_QK + half:(hh + 1) * RET_QK]
            rkt_ref[0, hh * RET_QK:hh * RET_QK + half, :] = ((x1 * crt - x2 * srt) * k_scale).astype(BF16)
            rkt_ref[0, hh * RET_QK + half:(hh + 1) * RET_QK, :] = ((x2 * crt + x1 * srt) * k_scale).astype(BF16)
        cm = cm_ref[...]
        sm = sm_ref[...]
        kr = _rope_half_roll(kr, cm, sm)
        cmt = cmt_ref[...]
        smt = smt_ref[...]
        hp = MLA_HEAD_PAD // 2
    for hh in range(MLA_HEADS):
        sl = slice(hh * MLA_HEAD_PAD, (hh + 1) * MLA_HEAD_PAD)
        km_ref[0, :, sl] = (kv[:, sl] + kr).astype(BF16)
        if not is_ctx:
            x1 = qt[sl.start:sl.start + hp]
            x2 = qt[sl.start + hp:sl.stop]
            qmt_ref[0, sl.start:sl.start + hp, :] = ((x1 * cmt[:hp] + x2 * smt[:hp]) * q_scale).astype(BF16)
            qmt_ref[0, sl.start + hp:sl.stop, :] = ((x2 * cmt[hp:] + x1 * smt[hp:]) * q_scale).astype(BF16)


def _in_proj(x, mod, w_groups, wrkt, qn, wuq_p, kvn, wuk_p, wuvt, tables, *, is_ctx, tm):
    B, L, D = x.shape
    w_specs = [pl.BlockSpec(w.shape, lambda b, i: (0, 0), pipeline_mode=pl.Buffered(1)) for w in w_groups]
    q_scale = (MLA_NOPE + MLA_ROPE) ** -0.5 * math.log2(math.e)
    kw = MLA_HEADS * MLA_HEAD_PAD
    vw = MLA_HEADS * MLA_V

    def const(shape):
        return pl.BlockSpec(shape, lambda b, i: (0,) * len(shape), pipeline_mode=pl.Buffered(1))

    def rows(width):
        return pl.BlockSpec((1, tm, width), lambda b, i: (b, i, 0))

    def out(width):
        return jax.ShapeDtypeStruct((B, L, width), BF16)

    x_spec = rows(D)
    def cols(height):
        return pl.BlockSpec((1, height, tm), lambda b, i: (b, 0, i))

    def out_t(height):
        return jax.ShapeDtypeStruct((B, height, L), BF16)

    if is_ctx:
        mod_spec = pl.BlockSpec((1, 6, D), lambda b, i: (B, 0, 0))
        in_specs = [x_spec, mod_spec, *w_specs, const(wrkt.shape), const(kvn.shape),
                    const(wuk_p.shape), const(wuvt.shape)]
        args = (x, mod, *w_groups, wrkt, kvn, wuk_p, wuvt)
        out_specs = [cols(_RQ), rows(_RV), rows(kw), cols(vw)]
        out_shape = [out_t(_RQ), out(_RV), out(kw), out_t(vw)]
    else:
        mod_spec = pl.BlockSpec((1, 6, D), lambda b, i: (b, 0, 0))
        tab = pl.BlockSpec((tm, LANES), lambda b, i: (i, 0))
        tab_t = pl.BlockSpec((RET_QK // 2, tm), lambda b, i: (0, i))
        tab_mt = pl.BlockSpec((MLA_HEAD_PAD, tm), lambda b, i: (0, i))
        in_specs = [x_spec, mod_spec, *w_specs, const(wrkt.shape), const(qn.shape), const(wuq_p.shape),
                    const(kvn.shape), const(wuk_p.shape), const(wuvt.shape),
                    tab, tab, tab_t, tab_t, tab, tab, tab_mt, tab_mt]
        args = (x, mod, *w_groups, wrkt, qn, wuq_p, kvn, wuk_p, wuvt) + tuple(tables)
        out_specs = [rows(_RQ), cols(_RQ), rows(_RV), rows(_RV), rows(D), rows(D), cols(kw), rows(kw), cols(vw)]
        out_shape = [out(_RQ), out_t(_RQ), out(_RV), out(_RV), out(D), out(D), out_t(kw), out(kw), out_t(vw)]
    return pl.pallas_call(
        functools.partial(_in_proj_kernel, is_ctx=is_ctx, q_scale=q_scale),
        grid=(B, L // tm),
        in_specs=in_specs,
        out_specs=out_specs,
        out_shape=out_shape,
        compiler_params=pltpu.CompilerParams(dimension_semantics=("arbitrary", "arbitrary"),
                                             vmem_limit_bytes=VMEM_LIMIT),
        name="in_proj_ctx" if is_ctx else "in_proj",
    )(*args)


def _log_sigmoid(x):
    return jnp.minimum(x, 0.0) - jnp.log(1.0 + jnp.exp(-jnp.abs(x)))


RET_LOOKAHEAD = 2


def _retention_kernel(dec_ref, q_ref, kt_ref, v_ref, kct_ref, vc_ref, o_ref, u_ref, st_ref, *, n_lat, n_ctx):
    C = RET_CHUNK
    lgf = _log_sigmoid(dec_ref[0, 0:1, :])
    lgb = _log_sigmoid(dec_ref[0, 1:2, :])
    ii = lax.broadcasted_iota(jnp.int32, (C, C), 0).astype(F32)
    jj = lax.broadcasted_iota(jnp.int32, (C, C), 1).astype(F32)
    rel = ii - jj
    mask = jnp.where(rel > 0, jnp.exp(rel * lgf), jnp.where(rel < 0, jnp.exp(-rel * lgb), 2.0))
    pos_col = ii[:, 0:1]
    pos_row = jj[0:1, :]
    lgf1 = lgf[:, 0:1]
    lgb1 = lgb[:, 0:1]
    qdec_f = jnp.exp((pos_col + 1.0) * lgf1)
    qdec_b = jnp.exp((C - pos_col) * lgb1)
    kdec_f = jnp.exp((C - 1.0 - pos_row) * lgf)
    kdec_b = jnp.exp(pos_row * lgb)
    cdec_f = jnp.exp(C * lgf1)
    cdec_b = jnp.exp(C * lgb1)

    def increment(ktref, vref, n):
        kt = ktref[0, :, n * C:(n + 1) * C].astype(F32)
        lhs = jnp.concatenate([(kt * kdec_f).astype(BF16), (kt * kdec_b).astype(BF16)], axis=0)
        return _dot(lhs, vref[0, n * C:(n + 1) * C, :])

    zero = jnp.zeros((RET_QK, RET_V), F32)
    s_f, s_b = zero, zero
    u_ctx = [increment(kct_ref, vc_ref, n) for n in range(n_ctx)]
    for n in range(n_ctx):
        s_f = s_f * cdec_f + u_ctx[n][0:RET_QK]
        s_b = s_b * cdec_b + u_ctx[n_ctx - 1 - n][RET_QK:]

    for n in range(n_lat):
        u_ref[n] = increment(kt_ref, v_ref, n)

    for t in range(n_lat):
        nb = n_lat - 1 - t
        st_ref[t, 0:RET_QK, :] = s_f.astype(BF16)
        st_ref[nb, RET_QK:, :] = s_b.astype(BF16)
        s_f = s_f * cdec_f + u_ref[t, 0:RET_QK, :]
        s_b = s_b * cdec_b + u_ref[nb, RET_QK:, :]

    scores = {}

    def score(n):
        scores[n] = _dot(q_ref[0, n * C:(n + 1) * C, :], kt_ref[0, :, n * C:(n + 1) * C])

    def consume(n):
        rows = slice(n * C, (n + 1) * C)
        a = (scores.pop(n) * mask).astype(BF16)
        q = q_ref[0, rows, :].astype(F32)
        q_dec = jnp.concatenate([(q * qdec_f).astype(BF16), (q * qdec_b).astype(BF16)], axis=1)
        o = _dot(a, v_ref[0, rows, :]) + _dot(q_dec, st_ref[n])
        mu = jnp.mean(o, axis=-1, keepdims=True)
        oc = o - mu
        var = jnp.mean(oc * oc, axis=-1, keepdims=True)
        o_ref[0, rows, :] = (oc * lax.rsqrt(var + LN_EPS)).astype(BF16)

    for n in range(n_lat + RET_LOOKAHEAD):
        if n < n_lat:
            score(n)
        if n >= RET_LOOKAHEAD:
            consume(n - RET_LOOKAHEAD)


def _retention(dec, rq, rkt, rv, rkt_c, rv_c):
    B, L, _ = rq.shape
    Lc = rv_c.shape[1]
    n_lat, n_ctx = L // RET_CHUNK, Lc // RET_CHUNK
    return pl.pallas_call(
        functools.partial(_retention_kernel, n_lat=n_lat, n_ctx=n_ctx),
        grid=(B, RET_HEADS),
        in_specs=[pl.BlockSpec((1, 2, LANES), lambda b, h: (h, 0, 0)),
                  pl.BlockSpec((1, L, RET_QK), lambda b, h: (b, 0, h)),
                  pl.BlockSpec((1, RET_QK, L), lambda b, h: (b, h, 0)),
                  pl.BlockSpec((1, L, RET_V), lambda b, h: (b, 0, h)),
                  pl.BlockSpec((1, RET_QK, Lc), lambda b, h: (b, h, 0)),
                  pl.BlockSpec((1, Lc, RET_V), lambda b, h: (b, 0, h))],
        out_specs=pl.BlockSpec((1, L, RET_V), lambda b, h: (b, 0, h)),
        out_shape=jax.ShapeDtypeStruct((B, L, RET_HEADS * RET_V), BF16),
        scratch_shapes=[pltpu.VMEM((n_lat, 2 * RET_QK, RET_V), F32),
                        pltpu.VMEM((n_lat, 2 * RET_QK, RET_V), BF16)],
        compiler_params=pltpu.CompilerParams(dimension_semantics=("arbitrary", "arbitrary"),
                                             vmem_limit_bytes=VMEM_LIMIT),
        name="retention",
    )(dec, rq, rkt, rv, rkt_c, rv_c)


ATT_SCORE_KEYS = 512
ATT_KEY_TILE = 512
ATT_Q_COLS = 256
ATT_ONES_ROWS = 16
ATT_MAX_ROWS = 64
ATT_LOOKAHEAD = 3


def _col_max(s):
    part = s[0:ATT_MAX_ROWS]
    for r in range(ATT_MAX_ROWS, s.shape[0], ATT_MAX_ROWS):
        part = jnp.maximum(part, s[r:r + ATT_MAX_ROWS])
    return jnp.max(part, axis=0, keepdims=True)


def _attention_kernel(q_ref, k_ref, vt_ref, kc_ref, vtc_ref, o_ref, *, l_lat, l_ctx):
    groups = [(kc_ref, vtc_ref, lo, min(ATT_SCORE_KEYS, l_ctx - lo)) for lo in range(0, l_ctx, ATT_SCORE_KEYS)]
    groups += [(k_ref, vt_ref, lo, ATT_SCORE_KEYS) for lo in range(0, l_lat, ATT_SCORE_KEYS)]
    tq = q_ref.shape[2]
    streams = [(hh, qb) for hh in range(2) for qb in range(tq // ATT_Q_COLS)]
    hs = [slice(hh * MLA_HEAD_PAD, (hh + 1) * MLA_HEAD_PAD) for hh in range(2)]
    vs = [slice(hh * MLA_V, (hh + 1) * MLA_V) for hh in range(2)]
    qs = [slice(qb * ATT_Q_COLS, (qb + 1) * ATT_Q_COLS) for qb in range(tq // ATT_Q_COLS)]
    m = {st: None for st in streams}
    acc = {st: None for st in streams}
    items = [(g, st, off, min(ATT_KEY_TILE, grp[3] - off))
             for g, grp in enumerate(groups) for st in streams for off in range(0, grp[3], ATT_KEY_TILE)]
    scores = {}

    def score(g, st):
        kref, _, lo, n = groups[g]
        hh, qb = st
        scores[g, st] = _dot(kref[0, lo:lo + n, hs[hh]], q_ref[0, hs[hh], qs[qb]])

    def consume(g, st, off, n):
        _, vref, lo, _ = groups[g]
        hh, _ = st
        s = scores[g, st][off:off + n]
        t_max = _col_max(s)
        m_new = t_max if m[st] is None else jnp.maximum(m[st], t_max)
        p = jnp.exp2(s - m_new).astype(BF16)
        ones = jnp.ones((ATT_ONES_ROWS, n), BF16)
        part = _dot(jnp.concatenate([vref[0, vs[hh], lo + off:lo + off + n], ones], axis=0), p)
        acc[st] = part if m[st] is None else acc[st] * jnp.exp2(m[st] - m_new) + part
        m[st] = m_new

    for i in range(len(items) + ATT_LOOKAHEAD):
        if i < len(items) and items[i][2] == 0:
            score(items[i][0], items[i][1])
        if i >= ATT_LOOKAHEAD:
            consume(*items[i - ATT_LOOKAHEAD])
    for hh, qb in streams:
        a = acc[hh, qb]
        o_ref[0, vs[hh], qs[qb]] = (a[0:MLA_V] / a[MLA_V:MLA_V + 1]).astype(BF16)


def _attention(qmt, km, vmt, km_c, vmt_c, *, tq):
    B, L, _ = km.shape
    Lc = km_c.shape[1]
    assert L % ATT_SCORE_KEYS == 0 and ATT_SCORE_KEYS % ATT_KEY_TILE == 0 and Lc % LANES == 0
    pairs = MLA_HEADS // 2
    qk_w = 2 * MLA_HEAD_PAD
    v_w = 2 * MLA_V
    return pl.pallas_call(
        functools.partial(_attention_kernel, l_lat=L, l_ctx=Lc),
        grid=(B, pairs, L // tq),
        in_specs=[pl.BlockSpec((1, qk_w, tq), lambda b, p, i: (b, p, i)),
                  pl.BlockSpec((1, L, qk_w), lambda b, p, i: (b, 0, p)),
                  pl.BlockSpec((1, v_w, L), lambda b, p, i: (b, p, 0)),
                  pl.BlockSpec((1, Lc, qk_w), lambda b, p, i: (b, 0, p)),
                  pl.BlockSpec((1, v_w, Lc), lambda b, p, i: (b, p, 0))],
        out_specs=pl.BlockSpec((1, v_w, tq), lambda b, p, i: (b, p, i)),
        out_shape=jax.ShapeDtypeStruct((B, MLA_HEADS * MLA_V, L), BF16),
        compiler_params=pltpu.CompilerParams(dimension_semantics=("arbitrary", "arbitrary", "arbitrary"),
                                             vmem_limit_bytes=VMEM_LIMIT),
        name="attention",
    )(qmt, km, vmt, km_c, vmt_c)


ROW_SUBTILE = 256


def _row_subtiles(rows):
    return [slice(lo, lo + ROW_SUBTILE) for lo in range(0, rows, ROW_SUBTILE)]


def _merge_kernel(x_ref, mod_ref, o_ret_ref, srg_ref, att_ref, sgr_ref, sgm_ref,
                  wro_ref, wmo_ref, wout_ref, g_ref, b_ref, y_ref):
    g1 = mod_ref[0, 2:3, :]
    subs = _row_subtiles(x_ref.shape[1])
    branch = []
    for r in subs:
        ret_in = (srg_ref[0, r, :].astype(F32) * o_ret_ref[0, r, :].astype(F32)).astype(BF16)
        ret = _dot(ret_in, wro_ref[...])
        mla = _dot_tn(att_ref[0, :, r], wmo_ref[...])
        branch.append((ret, mla))
    ys = []
    for r, (ret, mla) in zip(subs, branch):
        mix = (sgr_ref[0, r, :].astype(F32) * ret + sgm_ref[0, r, :].astype(F32) * mla).astype(BF16)
        ys.append(_dot(mix, wout_ref[...]))
    for r, y in zip(subs, ys):
        y_ref[0, r, :] = _layer_norm(ALPHA * x_ref[0, r, :] + g1 * y, g_ref[...], b_ref[...])


def _merge(x, mod, o_ret, srg, att, sgr, sgm, wro, wmo, wout, ln_g, ln_b, *, tm):
    B, L, D = x.shape

    def const(shape):
        return pl.BlockSpec(shape, lambda b, i: (0,) * len(shape), pipeline_mode=pl.Buffered(1))

    def rows(width):
        return pl.BlockSpec((1, tm, width), lambda b, i: (b, i, 0))

    return pl.pallas_call(
        _merge_kernel,
        grid=(B, L // tm),
        in_specs=[rows(D), pl.BlockSpec((1, 6, D), lambda b, i: (b, 0, 0)),
                  rows(o_ret.shape[2]), rows(srg.shape[2]),
                  pl.BlockSpec((1, att.shape[1], tm), lambda b, i: (b, 0, i)), rows(D), rows(D),
                  const(wro.shape), const(wmo.shape), const(wout.shape), const(ln_g.shape), const(ln_b.shape)],
        out_specs=rows(D),
        out_shape=jax.ShapeDtypeStruct((B, L, D), F32),
        compiler_params=pltpu.CompilerParams(dimension_semantics=("arbitrary", "arbitrary"),
                                             vmem_limit_bytes=VMEM_LIMIT),
        name="merge",
    )(x, mod, o_ret, srg, att, sgr, sgm, wro, wmo, wout, ln_g, ln_b)


def _ffn_kernel(x_ref, mod_ref, wgu_ref, wdn_ref, g_ref, b_ref, y_ref):
    sh2 = mod_ref[0, 3:4, :]
    s2 = mod_ref[0, 4:5, :]
    g2 = mod_ref[0, 5:6, :]
    subs = _row_subtiles(x_ref.shape[1])
    gate_up = []
    for r in subs:
        h = (x_ref[0, r, :] * (1.0 + s2) + sh2).astype(BF16)
        gate_up.append((_dot(h, wgu_ref[:, 0:D_FF]), _dot(h, wgu_ref[:, D_FF:2 * D_FF])))
    fs = [_dot((_silu(a) * b).astype(BF16), wdn_ref[...]) for a, b in gate_up]
    for r, f in zip(subs, fs):
        y_ref[0, r, :] = _layer_norm(ALPHA * x_ref[0, r, :] + g2 * f, g_ref[...], b_ref[...])


def _ffn(x, mod, wgu, wdn, ln_g, ln_b, *, tm):
    B, L, D = x.shape

    def const(shape):
        return pl.BlockSpec(shape, lambda b, i: (0,) * len(shape), pipeline_mode=pl.Buffered(1))

    rows = pl.BlockSpec((1, tm, D), lambda b, i: (b, i, 0))
    return pl.pallas_call(
        _ffn_kernel,
        grid=(B, L // tm),
        in_specs=[rows, pl.BlockSpec((1, 6, D), lambda b, i: (b, 0, 0)),
                  const(wgu.shape), const(wdn.shape), const(ln_g.shape), const(ln_b.shape)],
        out_specs=rows,
        out_shape=jax.ShapeDtypeStruct((B, L, D), F32),
        compiler_params=pltpu.CompilerParams(dimension_semantics=("arbitrary", "arbitrary"),
                                             vmem_limit_bytes=VMEM_LIMIT),
        name="ffn",
    )(x, mod, wgu, wdn, ln_g, ln_b)


def _rope_tables(L):
    t = np.arange(L, dtype=np.float64)
    half = RET_QK // 2
    fr = RET_ROPE_BASE ** (-np.arange(half, dtype=np.float64) / half)
    ang = t[:, None] * fr[None, :]
    cos_r = np.concatenate([np.cos(ang), np.cos(ang)], -1)
    sin_r = np.concatenate([-np.sin(ang), np.sin(ang)], -1)

    row = np.floor(t / GRID_W)
    col = t - row * GRID_W
    fm = ROPE_BASE ** (-np.arange(_RH, dtype=np.float64) / _RH)
    ang_rc = np.concatenate([row[:, None] * fm[None, :], col[:, None] * fm[None, :]], -1)
    cos_m = np.ones((L, LANES))
    sin_m = np.zeros((L, LANES))
    cos_m[:, _ROPE_X1:_ROPE_X1 + 2 * _RH] = np.cos(ang_rc)
    cos_m[:, _ROPE_X2:_ROPE_X2 + 2 * _RH] = np.cos(ang_rc)
    sin_m[:, _ROPE_X1:_ROPE_X1 + 2 * _RH] = -np.sin(ang_rc)
    sin_m[:, _ROPE_X2:_ROPE_X2 + 2 * _RH] = np.sin(ang_rc)
    cos_t = np.ascontiguousarray(np.cos(ang).T)
    sin_t = np.ascontiguousarray(np.sin(ang).T)
    cos_mt = np.ascontiguousarray(cos_m.T)
    sin_mt = np.ascontiguousarray(sin_m.T)
    return tuple(jnp.asarray(a, F32) for a in (cos_r, sin_r, cos_t, sin_t, cos_m, sin_m, cos_mt, sin_mt))


def _nope_lane(d):
    return _NOPE_A + d if d < _NOPE_SPLIT else _NOPE_B + d - _NOPE_SPLIT


def _rope_lane(r):
    axis, part, f = r // (2 * _RH), (r // _RH) % 2, r % _RH
    return (_ROPE_X2 if part else _ROPE_X1) + axis * _RH + f


def _placement(n_src, n_dst, pairs):
    p = np.zeros((n_src, n_dst), np.float32)
    for s, d in pairs:
        p[s, d] = 1.0
    return jnp.asarray(p)


def _select(a, b, dims):
    return lax.dot_general(a, b, (dims, ((), ())), precision=lax.Precision.HIGHEST, preferred_element_type=F32)


def _prep_weights(w_in, w_uq, w_ukv):
    hq, hkv, hp = MLA_NOPE + MLA_ROPE, MLA_NOPE + MLA_V, MLA_HEAD_PAD
    heads = range(MLA_HEADS)
    p_q = _placement(MLA_HEADS * hq, MLA_HEADS * hp,
                     [(h * hq + d, h * hp + _nope_lane(d)) for h in heads for d in range(MLA_NOPE)]
                     + [(h * hq + MLA_NOPE + r, h * hp + _rope_lane(r)) for h in heads for r in range(MLA_ROPE)])
    p_k = _placement(MLA_HEADS * hkv, MLA_HEADS * hp,
                     [(h * hkv + d, h * hp + _nope_lane(d)) for h in heads for d in range(MLA_NOPE)])
    p_v = _placement(MLA_HEADS * hkv, MLA_HEADS * MLA_V,
                     [(h * hkv + MLA_NOPE + e, h * MLA_V + e) for h in heads for e in range(MLA_V)])
    p_kr = _placement(MLA_ROPE, LANES, [(r, _rope_lane(r)) for r in range(MLA_ROPE)])

    w_a = w_in[:, :_KR_OFF].astype(BF16)
    w_kr = _select(w_in[:, _KR_OFF:_KR_OFF + MLA_ROPE], p_kr, ((1,), (0,))).astype(BF16)
    w_g = w_in[:, _KR_OFF + MLA_ROPE:].astype(BF16)
    wrkt = w_in[:, _COLS["rk"][1]:_COLS["rk"][2]].T.astype(BF16)
    wuq_pt = _select(p_q, w_uq, ((0,), (1,))).astype(BF16)
    wuk_p = _select(w_ukv, p_k, ((1,), (0,))).astype(BF16)
    wuvt = _select(p_v, w_ukv, ((0,), (1,))).astype(BF16)
    return (w_a, w_kr, w_g), wrkt, wuq_pt, wuk_p, wuvt


def kernel(x, c, ctx, c_ctx, w_ada, b_ada, w_in, ret_decay_f, ret_decay_b, w_ret_o, mla_q_norm, w_uq,
           mla_kv_norm, w_ukv, w_mla_o, w_out, ln1_g, ln1_b, w_gu, w_down, ln2_g, ln2_b):
    B, L, D = x.shape
    assert w_ada.shape[0] == DEPTH == 1
    i = 0

    c_rows = jnp.concatenate([c, c_ctx[None, :], jnp.zeros((8 - B - 1, D), F32)], axis=0)
    mod = _ada(c_rows, w_ada[i], b_ada[i]).reshape(8, 6, D)
    mod_l = mod_c = mod

    w_groups, wrkt, wuq_p, wuk_p, wuvt = _prep_weights(w_in[i], w_uq[i], w_ukv[i])
    qn = mla_q_norm[i].reshape(1, -1)
    kvn = mla_kv_norm[i].reshape(1, -1)
    tables = _rope_tables(L)

    rq, rk, rv, srg, sgr, sgm, qm, km, vm = _in_proj(
        x, mod_l, w_groups, wrkt, qn, wuq_p, kvn, wuk_p, wuvt, tables, is_ctx=False, tm=256)
    rk_c, rv_c, km_c, vm_c = _in_proj(
        ctx, mod_c, w_groups, wrkt, None, None, kvn, wuk_p, wuvt, None, is_ctx=True, tm=256)

    dec = jnp.stack([jnp.broadcast_to(ret_decay_f[i][:, None], (RET_HEADS, LANES)),
                     jnp.broadcast_to(ret_decay_b[i][:, None], (RET_HEADS, LANES))], axis=1)
    o_ret = _retention(dec, rq, rk, rv, rk_c, rv_c)
    att = _attention(qm, km, vm, km_c, vm_c, tq=1024)

    x1 = _merge(x, mod_l, o_ret, srg, att, sgr, sgm,
                w_ret_o[i].astype(BF16), w_mla_o[i].astype(BF16), w_out[i].astype(BF16),
                ln1_g[i].reshape(1, D), ln1_b[i].reshape(1, D), tm=512)
    return _ffn(x1, mod_l, w_gu[i].astype(BF16), w_down[i].astype(BF16),
                ln2_g[i].reshape(1, D), ln2_b[i].reshape(1, D), tm=512)
```

```python
import functools
import math

import numpy as np
import jax
import jax.numpy as jnp
from jax import lax
from jax.experimental import pallas as pl
from jax.experimental.pallas import tpu as pltpu

F32 = jnp.float32
BF16 = jnp.bfloat16

D_MODEL = 1024
GRID_W = 64
RET_HEADS = 4
RET_QK = 128
RET_V = 256
RET_CHUNK = 128
RET_ROPE_BASE = 10000.0
MLA_HEADS = 8
MLA_NOPE = 64
MLA_ROPE = 32
MLA_V = 64
MLA_Q_RANK = 384
MLA_KV_RANK = 256
ROPE_BASE = 10000.0
D_FF = -(-8 * D_MODEL // (3 * 256)) * 256
LN_EPS = 1e-5
RMS_EPS = 1e-6
DEPTH = 1
ALPHA = (2.0 * DEPTH) ** 0.25

LANES = 128
MLA_HEAD_PAD = LANES
VMEM_LIMIT = 56 * 1024 * 1024

_RQ = RET_HEADS * RET_QK
_RV = RET_HEADS * RET_V
_KR_OFF = 2 * _RQ + 2 * _RV + MLA_Q_RANK + MLA_KV_RANK
_COLS = {
    "rq": (0, 0, _RQ),
    "rk": (0, _RQ, 2 * _RQ),
    "rv": (0, 2 * _RQ, 2 * _RQ + _RV),
    "rg": (0, 2 * _RQ + _RV, 2 * _RQ + 2 * _RV),
    "dq": (0, 2 * _RQ + 2 * _RV, 2 * _RQ + 2 * _RV + MLA_Q_RANK),
    "dkv": (0, 2 * _RQ + 2 * _RV + MLA_Q_RANK, _KR_OFF),
    "kr": (1, 0, LANES),
    "gr": (2, 0, D_MODEL),
    "gm": (2, D_MODEL, 2 * D_MODEL),
}

_RH = MLA_ROPE // 4


def _dot(a, b):
    return jnp.dot(a, b, preferred_element_type=F32)


def _dot_nt(a, b):
    return lax.dot_general(a, b, (((1,), (1,)), ((), ())), preferred_element_type=F32)


def _dot_tn(a, b):
    return lax.dot_general(a, b, (((0,), (0,)), ((), ())), preferred_element_type=F32)


def _sigmoid(x):
    return 1.0 / (1.0 + jnp.exp(-x))


def _silu(x):
    return x * _sigmoid(x)


def _layer_norm(x, g, b):
    mu = jnp.mean(x, axis=-1, keepdims=True)
    xc = x - mu
    var = jnp.mean(xc * xc, axis=-1, keepdims=True)
    return xc * lax.rsqrt(var + LN_EPS) * g + b


def _rope_half_roll(x, cos, sin_signed):
    return x * cos + pltpu.roll(x, LANES // 2, 1) * sin_signed


def _ada_kernel(c_ref, w_ref, b_ref, o_ref):
    sc = _silu(c_ref[...]).astype(BF16)
    o_ref[...] = _dot(sc, w_ref[...].astype(BF16)) + b_ref[...]


def _ada(c_rows, w_ada, b_ada):
    rows, d = c_rows.shape
    n = w_ada.shape[1]
    tn = 1536
    return pl.pallas_call(
        _ada_kernel,
        grid=(n // tn,),
        in_specs=[pl.BlockSpec((rows, d), lambda j: (0, 0)),
                  pl.BlockSpec((d, tn), lambda j: (0, j)),
                  pl.BlockSpec((1, tn), lambda j: (0, j))],
        out_specs=pl.BlockSpec((rows, tn), lambda j: (0, j)),
        out_shape=jax.ShapeDtypeStruct((rows, n), F32),
        compiler_params=pltpu.CompilerParams(dimension_semantics=("arbitrary",), vmem_limit_bytes=VMEM_LIMIT),
        name="ada",
    )(c_rows, w_ada, b_ada.reshape(1, n))


def _in_proj_kernel(*refs, is_ctx, q_scale):
    if is_ctx:
        (x_ref, mod_ref, wa_ref, wkr_ref, wg_ref, kvn_ref, wuk_ref, wuvt_ref,
         rkt_ref, rv_ref, km_ref, vmt_ref) = refs
    else:
        (x_ref, mod_ref, wa_ref, wkr_ref, wg_ref, qn_ref, wuqt_ref, kvn_ref, wuk_ref, wuvt_ref,
         cr_ref, sr_ref, crt_ref, srt_ref, cm_ref, sma_ref, smb_ref, tqt_ref,
         rq_ref, rkt_ref, rv_ref, srg_ref, sgr_ref, sgm_ref, qmt_ref, km_ref, vmt_ref) = refs

    x = x_ref[0]
    sh1 = mod_ref[0, 0:1, :]
    s1 = mod_ref[0, 1:2, :]
    h = (x * (1.0 + s1) + sh1).astype(BF16)

    def proj(name):
        group, lo, hi = _COLS[name]
        return _dot(h, (wa_ref, wkr_ref, wg_ref)[group][:, lo:hi])

    def rms(v, g):
        return v * lax.rsqrt(jnp.mean(v * v, axis=-1, keepdims=True) + RMS_EPS) * g

    k_scale = RET_QK ** -0.5
    dkv_raw = proj("dkv")
    kr = proj("kr")
    if not is_ctx:
        dq_raw = proj("dq")
    rkt = proj("rk").T
    if not is_ctx:
        rq = proj("rq")

    dkv = rms(dkv_raw, kvn_ref[...]).astype(BF16)
    kv = _dot(dkv, wuk_ref[...])
    vmt_ref[0] = _dot_nt(wuvt_ref[...], dkv).astype(BF16)
    if not is_ctx:
        dq = rms(dq_raw, qn_ref[...]).astype(BF16)
        qt = _dot_nt(wuqt_ref[...], dq)
        srg_ref[0] = _silu(proj("rg")).astype(BF16)
        sgr_ref[0] = _sigmoid(proj("gr")).astype(BF16)
        sgm_ref[0] = _sigmoid(proj("gm")).astype(BF16)
    rv_ref[0] = proj("rv").astype(BF16)

    if is_ctx:
        rkt_ref[0] = (rkt * k_scale).astype(BF16)
    else:
        cr = cr_ref[...]
        sr = sr_ref[...]
        crt = crt_ref[...]
        srt = srt_ref[...]
        half = RET_QK // 2
        for hh in range(RET_HEADS):
            sl = slice(hh * RET_QK, (hh + 1) * RET_QK)
            rq_ref[0, :, sl] = _rope_half_roll(rq[:, sl], cr, sr).astype(BF16)
            x1 = rkt[hh * RET_QK:hh * RET_QK + half]
            x2 = rkt[hh * RET_QK + half:(hh + 1) * RET_QK]
            rkt_ref[0, hh * RET_QK:hh * RET_QK + half, :] = ((x1 * crt - x2 * srt) * k_scale).astype(BF16)
            rkt_ref[0, hh * RET_QK + half:(hh + 1) * RET_QK, :] = ((x2 * crt + x1 * srt) * k_scale).astype(BF16)
        kr = (kr * cm_ref[...] + pltpu.roll(kr, LANES - _RH, 1) * sma_ref[...]
              + pltpu.roll(kr, _RH, 1) * smb_ref[...])
        tqt = tqt_ref[...]
        cos_ax = (tqt[0:_RH], tqt[2 * _RH:3 * _RH])
        sin_ax = (tqt[_RH:2 * _RH], tqt[3 * _RH:4 * _RH])
    for hh in range(MLA_HEADS):
        sl = slice(hh * MLA_HEAD_PAD, (hh + 1) * MLA_HEAD_PAD)
        km_ref[0, :, sl] = (kv[:, sl] + kr).astype(BF16)
        if not is_ctx:
            slabs = [qt[sl.start:sl.start + MLA_NOPE]]
            for ax in range(2):
                lo = sl.start + MLA_NOPE + 2 * ax * _RH
                x1, x2 = qt[lo:lo + _RH], qt[lo + _RH:lo + 2 * _RH]
                slabs += [x1 * cos_ax[ax] - x2 * sin_ax[ax], x2 * cos_ax[ax] + x1 * sin_ax[ax]]
            slabs.append(qt[sl.start + MLA_NOPE + MLA_ROPE:sl.stop])
            qmt_ref[0, sl, :] = (jnp.concatenate(slabs, axis=0) * q_scale).astype(BF16)


def _in_proj(x, mod, w_groups, qn, wuq_p, kvn, wuk_p, wuvt, tables, *, is_ctx, tm):
    B, L, D = x.shape
    w_specs = [pl.BlockSpec(w.shape, lambda b, i: (0, 0), pipeline_mode=pl.Buffered(1)) for w in w_groups]
    q_scale = (MLA_NOPE + MLA_ROPE) ** -0.5 * math.log2(math.e)
    kw = MLA_HEADS * MLA_HEAD_PAD
    vw = MLA_HEADS * MLA_V

    def const(shape):
        return pl.BlockSpec(shape, lambda b, i: (0,) * len(shape), pipeline_mode=pl.Buffered(1))

    def rows(width):
        return pl.BlockSpec((1, tm, width), lambda b, i: (b, i, 0))

    def out(width):
        return jax.ShapeDtypeStruct((B, L, width), BF16)

    x_spec = rows(D)
    def cols(height):
        return pl.BlockSpec((1, height, tm), lambda b, i: (b, 0, i))

    def out_t(height):
        return jax.ShapeDtypeStruct((B, height, L), BF16)

    if is_ctx:
        mod_spec = pl.BlockSpec((1, 6, D), lambda b, i: (B, 0, 0))
        in_specs = [x_spec, mod_spec, *w_specs, const(kvn.shape), const(wuk_p.shape), const(wuvt.shape)]
        args = (x, mod, *w_groups, kvn, wuk_p, wuvt)
        out_specs = [cols(_RQ), rows(_RV), rows(kw), cols(vw)]
        out_shape = [out_t(_RQ), out(_RV), out(kw), out_t(vw)]
    else:
        mod_spec = pl.BlockSpec((1, 6, D), lambda b, i: (b, 0, 0))
        tab = pl.BlockSpec((tm, LANES), lambda b, i: (i, 0))
        tab_t = pl.BlockSpec((RET_QK // 2, tm), lambda b, i: (0, i))
        tab_qt = pl.BlockSpec((MLA_ROPE, tm), lambda b, i: (0, i))
        in_specs = [x_spec, mod_spec, *w_specs, const(qn.shape), const(wuq_p.shape),
                    const(kvn.shape), const(wuk_p.shape), const(wuvt.shape),
                    tab, tab, tab_t, tab_t, tab, tab, tab, tab_qt]
        args = (x, mod, *w_groups, qn, wuq_p, kvn, wuk_p, wuvt) + tuple(tables)
        out_specs = [rows(_RQ), cols(_RQ), rows(_RV), rows(_RV), rows(D), rows(D), cols(kw), rows(kw), cols(vw)]
        out_shape = [out(_RQ), out_t(_RQ), out(_RV), out(_RV), out(D), out(D), out_t(kw), out(kw), out_t(vw)]
    return pl.pallas_call(
        functools.partial(_in_proj_kernel, is_ctx=is_ctx, q_scale=q_scale),
        grid=(B, L // tm),
        in_specs=in_specs,
        out_specs=out_specs,
        out_shape=out_shape,
        compiler_params=pltpu.CompilerParams(dimension_semantics=("arbitrary", "arbitrary"),
                                             vmem_limit_bytes=VMEM_LIMIT),
        name="in_proj_ctx" if is_ctx else "in_proj",
    )(*args)


def _log_sigmoid(x):
    return jnp.minimum(x, 0.0) - jnp.log(1.0 + jnp.exp(-jnp.abs(x)))


RET_LOOKAHEAD = 2


def _retention_kernel(dec_ref, q_ref, kt_ref, v_ref, kct_ref, vc_ref, o_ref, u_ref, st_ref, *, n_lat, n_ctx):
    C = RET_CHUNK
    lgf = _log_sigmoid(dec_ref[0, 0:1, :])
    lgb = _log_sigmoid(dec_ref[0, 1:2, :])
    ii = lax.broadcasted_iota(jnp.int32, (C, C), 0).astype(F32)
    jj = lax.broadcasted_iota(jnp.int32, (C, C), 1).astype(F32)
    rel = ii - jj
    mask = jnp.where(rel > 0, jnp.exp(rel * lgf), jnp.where(rel < 0, jnp.exp(-rel * lgb), 2.0))
    pos_col = ii[:, 0:1]
    pos_row = jj[0:1, :]
    lgf1 = lgf[:, 0:1]
    lgb1 = lgb[:, 0:1]
    qdec_f = jnp.exp((pos_col + 1.0) * lgf1)
    qdec_b = jnp.exp((C - pos_col) * lgb1)
    kdec_f = jnp.exp((C - 1.0 - pos_row) * lgf)
    kdec_b = jnp.exp(pos_row * lgb)
    cdec_f = jnp.exp(C * lgf1)
    cdec_b = jnp.exp(C * lgb1)

    def increment(ktref, vref, n):
        kt = ktref[0, :, n * C:(n + 1) * C].astype(F32)
        lhs = jnp.concatenate([(kt * kdec_f).astype(BF16), (kt * kdec_b).astype(BF16)], axis=0)
        return _dot(lhs, vref[0, n * C:(n + 1) * C, :])

    zero = jnp.zeros((RET_QK, RET_V), F32)
    s_f, s_b = zero, zero
    u_ctx = [increment(kct_ref, vc_ref, n) for n in range(n_ctx)]
    for n in range(n_ctx):
        s_f = s_f * cdec_f + u_ctx[n][0:RET_QK]
        s_b = s_b * cdec_b + u_ctx[n_ctx - 1 - n][RET_QK:]

    for n in range(n_lat):
        u_ref[n] = increment(kt_ref, v_ref, n)

    for t in range(n_lat):
        nb = n_lat - 1 - t
        st_ref[t, 0:RET_QK, :] = s_f.astype(BF16)
        st_ref[nb, RET_QK:, :] = s_b.astype(BF16)
        s_f = s_f * cdec_f + u_ref[t, 0:RET_QK, :]
        s_b = s_b * cdec_b + u_ref[nb, RET_QK:, :]

    scores = {}

    def score(n):
        scores[n] = _dot(q_ref[0, n * C:(n + 1) * C, :], kt_ref[0, :, n * C:(n + 1) * C])

    def consume(n):
        rows = slice(n * C, (n + 1) * C)
        a = (scores.pop(n) * mask).astype(BF16)
        q = q_ref[0, rows, :].astype(F32)
        q_dec = jnp.concatenate([(q * qdec_f).astype(BF16), (q * qdec_b).astype(BF16)], axis=1)
        o = _dot(a, v_ref[0, rows, :]) + _dot(q_dec, st_ref[n])
        mu = jnp.mean(o, axis=-1, keepdims=True)
        oc = o - mu
        var = jnp.mean(oc * oc, axis=-1, keepdims=True)
        o_ref[0, rows, :] = (oc * lax.rsqrt(var + LN_EPS)).astype(BF16)

    for n in range(n_lat + RET_LOOKAHEAD):
        if n < n_lat:
            score(n)
        if n >= RET_LOOKAHEAD:
            consume(n - RET_LOOKAHEAD)


def _retention(dec, rq, rkt, rv, rkt_c, rv_c):
    B, L, _ = rq.shape
    Lc = rv_c.shape[1]
    n_lat, n_ctx = L // RET_CHUNK, Lc // RET_CHUNK
    return pl.pallas_call(
        functools.partial(_retention_kernel, n_lat=n_lat, n_ctx=n_ctx),
        grid=(B, RET_HEADS),
        in_specs=[pl.BlockSpec((1, 2, LANES), lambda b, h: (h, 0, 0)),
                  pl.BlockSpec((1, L, RET_QK), lambda b, h: (b, 0, h)),
                  pl.BlockSpec((1, RET_QK, L), lambda b, h: (b, h, 0)),
                  pl.BlockSpec((1, L, RET_V), lambda b, h: (b, 0, h)),
                  pl.BlockSpec((1, RET_QK, Lc), lambda b, h: (b, h, 0)),
                  pl.BlockSpec((1, Lc, RET_V), lambda b, h: (b, 0, h))],
        out_specs=pl.BlockSpec((1, L, RET_V), lambda b, h: (b, 0, h)),
        out_shape=jax.ShapeDtypeStruct((B, L, RET_HEADS * RET_V), BF16),
        scratch_shapes=[pltpu.VMEM((n_lat, 2 * RET_QK, RET_V), F32),
                        pltpu.VMEM((n_lat, 2 * RET_QK, RET_V), BF16)],
        compiler_params=pltpu.CompilerParams(dimension_semantics=("arbitrary", "arbitrary"),
                                             vmem_limit_bytes=VMEM_LIMIT),
        name="retention",
    )(dec, rq, rkt, rv, rkt_c, rv_c)


ATT_SCORE_KEYS = 512
ATT_KEY_TILE = 512
ATT_Q_COLS = 256
ATT_ONES_ROWS = 16
ATT_MAX_ROWS = 64
ATT_LOOKAHEAD = 3


def _col_max(s):
    part = s[0:ATT_MAX_ROWS]
    for r in range(ATT_MAX_ROWS, s.shape[0], ATT_MAX_ROWS):
        part = jnp.maximum(part, s[r:r + ATT_MAX_ROWS])
    return jnp.max(part, axis=0, keepdims=True)


def _attention_kernel(q_ref, k_ref, vt_ref, kc_ref, vtc_ref, o_ref, *, l_lat, l_ctx):
    groups = [(kc_ref, vtc_ref, lo, min(ATT_SCORE_KEYS, l_ctx - lo)) for lo in range(0, l_ctx, ATT_SCORE_KEYS)]
    groups += [(k_ref, vt_ref, lo, ATT_SCORE_KEYS) for lo in range(0, l_lat, ATT_SCORE_KEYS)]
    tq = q_ref.shape[2]
    streams = [(hh, qb) for hh in range(2) for qb in range(tq // ATT_Q_COLS)]
    hs = [slice(hh * MLA_HEAD_PAD, (hh + 1) * MLA_HEAD_PAD) for hh in range(2)]
    vs = [slice(hh * MLA_V, (hh + 1) * MLA_V) for hh in range(2)]
    qs = [slice(qb * ATT_Q_COLS, (qb + 1) * ATT_Q_COLS) for qb in range(tq // ATT_Q_COLS)]
    m = {st: None for st in streams}
    acc = {st: None for st in streams}
    items = [(g, st, off, min(ATT_KEY_TILE, grp[3] - off))
             for g, grp in enumerate(groups) for st in streams for off in range(0, grp[3], ATT_KEY_TILE)]
    scores = {}

    def score(g, st):
        kref, _, lo, n = groups[g]
        hh, qb = st
        scores[g, st] = _dot(kref[0, lo:lo + n, hs[hh]], q_ref[0, hs[hh], qs[qb]])

    def consume(g, st, off, n):
        _, vref, lo, _ = groups[g]
        hh, _ = st
        s = scores[g, st][off:off + n]
        t_max = _col_max(s)
        m_new = t_max if m[st] is None else jnp.maximum(m[st], t_max)
        p = jnp.exp2(s - m_new).astype(BF16)
        ones = jnp.ones((ATT_ONES_ROWS, n), BF16)
        part = _dot(jnp.concatenate([vref[0, vs[hh], lo + off:lo + off + n], ones], axis=0), p)
        acc[st] = part if m[st] is None else acc[st] * jnp.exp2(m[st] - m_new) + part
        m[st] = m_new

    for i in range(len(items) + ATT_LOOKAHEAD):
        if i < len(items) and items[i][2] == 0:
            score(items[i][0], items[i][1])
        if i >= ATT_LOOKAHEAD:
            consume(*items[i - ATT_LOOKAHEAD])
    for hh, qb in streams:
        a = acc[hh, qb]
        o_ref[0, vs[hh], qs[qb]] = (a[0:MLA_V] / a[MLA_V:MLA_V + 1]).astype(BF16)


def _attention(qmt, km, vmt, km_c, vmt_c, *, tq):
    B, L, _ = km.shape
    Lc = km_c.shape[1]
    assert L % ATT_SCORE_KEYS == 0 and ATT_SCORE_KEYS % ATT_KEY_TILE == 0 and Lc % LANES == 0
    pairs = MLA_HEADS // 2
    qk_w = 2 * MLA_HEAD_PAD
    v_w = 2 * MLA_V
    return pl.pallas_call(
        functools.partial(_attention_kernel, l_lat=L, l_ctx=Lc),
        grid=(B, pairs, L // tq),
        in_specs=[pl.BlockSpec((1, qk_w, tq), lambda b, p, i: (b, p, i)),
                  pl.BlockSpec((1, L, qk_w), lambda b, p, i: (b, 0, p)),
                  pl.BlockSpec((1, v_w, L), lambda b, p, i: (b, p, 0)),
                  pl.BlockSpec((1, Lc, qk_w), lambda b, p, i: (b, 0, p)),
                  pl.BlockSpec((1, v_w, Lc), lambda b, p, i: (b, p, 0))],
        out_specs=pl.BlockSpec((1, v_w, tq), lambda b, p, i: (b, p, i)),
        out_shape=jax.ShapeDtypeStruct((B, MLA_HEADS * MLA_V, L), BF16),
        compiler_params=pltpu.CompilerParams(dimension_semantics=("arbitrary", "arbitrary", "arbitrary"),
                                             vmem_limit_bytes=VMEM_LIMIT),
        name="attention",
    )(qmt, km, vmt, km_c, vmt_c)


ROW_SUBTILE = 256


def _row_subtiles(rows):
    return [slice(lo, lo + ROW_SUBTILE) for lo in range(0, rows, ROW_SUBTILE)]


def _merge_kernel(x_ref, mod_ref, o_ret_ref, srg_ref, att_ref, sgr_ref, sgm_ref,
                  wro_ref, wmo_ref, wout_ref, g_ref, b_ref, y_ref):
    g1 = mod_ref[0, 2:3, :]
    subs = _row_subtiles(x_ref.shape[1])
    branch = []
    for r in subs:
        ret_in = (srg_ref[0, r, :].astype(F32) * o_ret_ref[0, r, :].astype(F32)).astype(BF16)
        ret = _dot(ret_in, wro_ref[...])
        mla = _dot_tn(att_ref[0, :, r], wmo_ref[...])
        branch.append((ret, mla))
    ys = []
    for r, (ret, mla) in zip(subs, branch):
        mix = (sgr_ref[0, r, :].astype(F32) * ret + sgm_ref[0, r, :].astype(F32) * mla).astype(BF16)
        ys.append(_dot(mix, wout_ref[...]))
    for r, y in zip(subs, ys):
        y_ref[0, r, :] = _layer_norm(ALPHA * x_ref[0, r, :] + g1 * y, g_ref[...], b_ref[...])


def _merge(x, mod, o_ret, srg, att, sgr, sgm, wro, wmo, wout, ln_g, ln_b, *, tm):
    B, L, D = x.shape

    def const(shape):
        return pl.BlockSpec(shape, lambda b, i: (0,) * len(shape), pipeline_mode=pl.Buffered(1))

    def rows(width):
        return pl.BlockSpec((1, tm, width), lambda b, i: (b, i, 0))

    return pl.pallas_call(
        _merge_kernel,
        grid=(B, L // tm),
        in_specs=[rows(D), pl.BlockSpec((1, 6, D), lambda b, i: (b, 0, 0)),
                  rows(o_ret.shape[2]), rows(srg.shape[2]),
                  pl.BlockSpec((1, att.shape[1], tm), lambda b, i: (b, 0, i)), rows(D), rows(D),
                  const(wro.shape), const(wmo.shape), const(wout.shape), const(ln_g.shape), const(ln_b.shape)],
        out_specs=rows(D),
        out_shape=jax.ShapeDtypeStruct((B, L, D), F32),
        compiler_params=pltpu.CompilerParams(dimension_semantics=("arbitrary", "arbitrary"),
                                             vmem_limit_bytes=VMEM_LIMIT),
        name="merge",
    )(x, mod, o_ret, srg, att, sgr, sgm, wro, wmo, wout, ln_g, ln_b)


def _ffn_kernel(x_ref, mod_ref, wgu_ref, wdn_ref, g_ref, b_ref, y_ref):
    sh2 = mod_ref[0, 3:4, :]
    s2 = mod_ref[0, 4:5, :]
    g2 = mod_ref[0, 5:6, :]
    subs = _row_subtiles(x_ref.shape[1])
    gate_up = []
    for r in subs:
        h = (x_ref[0, r, :] * (1.0 + s2) + sh2).astype(BF16)
        gate_up.append((_dot(h, wgu_ref[:, 0:D_FF]), _dot(h, wgu_ref[:, D_FF:2 * D_FF])))
    fs = [_dot((_silu(a) * b).astype(BF16), wdn_ref[...]) for a, b in gate_up]
    for r, f in zip(subs, fs):
        y_ref[0, r, :] = _layer_norm(ALPHA * x_ref[0, r, :] + g2 * f, g_ref[...], b_ref[...])


def _ffn(x, mod, wgu, wdn, ln_g, ln_b, *, tm):
    B, L, D = x.shape

    def const(shape):
        return pl.BlockSpec(shape, lambda b, i: (0,) * len(shape), pipeline_mode=pl.Buffered(1))

    rows = pl.BlockSpec((1, tm, D), lambda b, i: (b, i, 0))
    return pl.pallas_call(
        _ffn_kernel,
        grid=(B, L // tm),
        in_specs=[rows, pl.BlockSpec((1, 6, D), lambda b, i: (b, 0, 0)),
                  const(wgu.shape), const(wdn.shape), const(ln_g.shape), const(ln_b.shape)],
        out_specs=rows,
        out_shape=jax.ShapeDtypeStruct((B, L, D), F32),
        compiler_params=pltpu.CompilerParams(dimension_semantics=("arbitrary", "arbitrary"),
                                             vmem_limit_bytes=VMEM_LIMIT),
        name="ffn",
    )(x, mod, wgu, wdn, ln_g, ln_b)


def _rope_tables(L):
    t = np.arange(L, dtype=np.float64)
    half = RET_QK // 2
    fr = RET_ROPE_BASE ** (-np.arange(half, dtype=np.float64) / half)
    ang = t[:, None] * fr[None, :]
    cos_r = np.concatenate([np.cos(ang), np.cos(ang)], -1)
    sin_r = np.concatenate([-np.sin(ang), np.sin(ang)], -1)

    row = np.floor(t / GRID_W)
    col = t - row * GRID_W
    fm = ROPE_BASE ** (-np.arange(_RH, dtype=np.float64) / _RH)
    cos_t = np.ascontiguousarray(np.cos(ang).T)
    sin_t = np.ascontiguousarray(np.sin(ang).T)

    ang_ax = (row[:, None] * fm[None, :], col[:, None] * fm[None, :])
    cos_k = np.ones((L, LANES))
    sin_up = np.zeros((L, LANES))
    sin_down = np.zeros((L, LANES))
    for ax in range(2):
        lo = MLA_NOPE + 2 * ax * _RH
        cos_k[:, lo:lo + 2 * _RH] = np.concatenate([np.cos(ang_ax[ax])] * 2, -1)
        sin_up[:, lo:lo + _RH] = -np.sin(ang_ax[ax])
        sin_down[:, lo + _RH:lo + 2 * _RH] = np.sin(ang_ax[ax])
    tab_qt = np.ascontiguousarray(np.concatenate(
        [np.cos(ang_ax[0]), np.sin(ang_ax[0]), np.cos(ang_ax[1]), np.sin(ang_ax[1])], -1).T)
    return tuple(jnp.asarray(a, F32) for a in (cos_r, sin_r, cos_t, sin_t, cos_k, sin_up, sin_down, tab_qt))


def _prep_weights(w_in, w_uq, w_ukv):
    hq, hkv, hp = MLA_NOPE + MLA_ROPE, MLA_NOPE + MLA_V, MLA_HEAD_PAD
    w_a = w_in[:, :_KR_OFF].astype(BF16)
    w_kr = jnp.pad(w_in[:, _KR_OFF:_KR_OFF + MLA_ROPE], ((0, 0), (MLA_NOPE, hp - hq))).astype(BF16)
    w_g = w_in[:, _KR_OFF + MLA_ROPE:].astype(BF16)

    def pad_heads(w, width):
        k = w.shape[0]
        return jnp.pad(w.reshape(k, MLA_HEADS, width), ((0, 0), (0, 0), (0, hp - width))).reshape(k, MLA_HEADS * hp)

    wuq_pt = pad_heads(w_uq, hq).astype(BF16).T
    kv3 = w_ukv.reshape(w_ukv.shape[0], MLA_HEADS, hkv)
    wuk_p = pad_heads(kv3[:, :, :MLA_NOPE].reshape(-1, MLA_HEADS * MLA_NOPE), MLA_NOPE).astype(BF16)
    wuvt = kv3[:, :, MLA_NOPE:].reshape(-1, MLA_HEADS * MLA_V).astype(BF16).T
    return (w_a, w_kr, w_g), wuq_pt, wuk_p, wuvt


def kernel(x, c, ctx, c_ctx, w_ada, b_ada, w_in, ret_decay_f, ret_decay_b, w_ret_o, mla_q_norm, w_uq,
           mla_kv_norm, w_ukv, w_mla_o, w_out, ln1_g, ln1_b, w_gu, w_down, ln2_g, ln2_b):
    B, L, D = x.shape
    assert w_ada.shape[0] == DEPTH == 1
    i = 0

    c_rows = jnp.concatenate([c, c_ctx[None, :], jnp.zeros((8 - B - 1, D), F32)], axis=0)
    mod = _ada(c_rows, w_ada[i], b_ada[i]).reshape(8, 6, D)
    mod_l = mod_c = mod

    w_groups, wuq_p, wuk_p, wuvt = _prep_weights(w_in[i], w_uq[i], w_ukv[i])
    qn = mla_q_norm[i].reshape(1, -1)
    kvn = mla_kv_norm[i].reshape(1, -1)
    tables = _rope_tables(L)

    rq, rk, rv, srg, sgr, sgm, qm, km, vm = _in_proj(
        x, mod_l, w_groups, qn, wuq_p, kvn, wuk_p, wuvt, tables, is_ctx=False, tm=256)
    rk_c, rv_c, km_c, vm_c = _in_proj(
        ctx, mod_c, w_groups, None, None, kvn, wuk_p, wuvt, None, is_ctx=True, tm=256)

    dec = jnp.stack([jnp.broadcast_to(ret_decay_f[i][:, None], (RET_HEADS, LANES)),
                     jnp.broadcast_to(ret_decay_b[i][:, None], (RET_HEADS, LANES))], axis=1)
    o_ret = _retention(dec, rq, rk, rv, rk_c, rv_c)
    att = _attention(qm, km, vm, km_c, vm_c, tq=1024)

    x1 = _merge(x, mod_l, o_ret, srg, att, sgr, sgm,
                w_ret_o[i].astype(BF16), w_mla_o[i].astype(BF16), w_out[i].astype(BF16),
                ln1_g[i].reshape(1, D), ln1_b[i].reshape(1, D), tm=512)
    return _ffn(x1, mod_l, w_gu[i].astype(BF16), w_down[i].astype(BF16),
                ln2_g[i].reshape(1, D), ln2_b[i].reshape(1, D), tm=512)
```

```python
import functools
import math

import numpy as np
import jax
import jax.numpy as jnp
from jax import lax
from jax.experimental import pallas as pl
from jax.experimental.pallas import tpu as pltpu

F32 = jnp.float32
BF16 = jnp.bfloat16

D_MODEL = 1024
GRID_W = 64
RET_HEADS = 4
RET_QK = 128
RET_V = 256
RET_CHUNK = 128
RET_ROPE_BASE = 10000.0
MLA_HEADS = 8
MLA_NOPE = 64
MLA_ROPE = 32
MLA_V = 64
MLA_Q_RANK = 384
MLA_KV_RANK = 256
ROPE_BASE = 10000.0
D_FF = -(-8 * D_MODEL // (3 * 256)) * 256
LN_EPS = 1e-5
RMS_EPS = 1e-6
DEPTH = 1
ALPHA = (2.0 * DEPTH) ** 0.25

LANES = 128
MLA_HEAD_PAD = LANES
VMEM_LIMIT = 56 * 1024 * 1024

_RQ = RET_HEADS * RET_QK
_RV = RET_HEADS * RET_V
_KR_OFF = 2 * _RQ + 2 * _RV + MLA_Q_RANK + MLA_KV_RANK
_COLS = {
    "rq": (0, 0, _RQ),
    "rk": (0, _RQ, 2 * _RQ),
    "rv": (0, 2 * _RQ, 2 * _RQ + _RV),
    "rg": (0, 2 * _RQ + _RV, 2 * _RQ + 2 * _RV),
    "dkv": (0, 2 * _RQ + 2 * _RV + MLA_Q_RANK, _KR_OFF),
    "dq": (1, 0, MLA_Q_RANK),
    "kr": (1, MLA_Q_RANK, MLA_Q_RANK + LANES),
    "gr": (2, 0, D_MODEL),
    "gm": (2, D_MODEL, 2 * D_MODEL),
}

_RH = MLA_ROPE // 4


def _dot(a, b):
    return jnp.dot(a, b, preferred_element_type=F32)


def _dot_nt(a, b):
    return lax.dot_general(a, b, (((1,), (1,)), ((), ())), preferred_element_type=F32)


def _dot_tn(a, b):
    return lax.dot_general(a, b, (((0,), (0,)), ((), ())), preferred_element_type=F32)


def _sigmoid(x):
    return 1.0 / (1.0 + jnp.exp(-x))


def _silu(x):
    return x * _sigmoid(x)


def _layer_norm(x, g, b):
    mu = jnp.mean(x, axis=-1, keepdims=True)
    xc = x - mu
    var = jnp.mean(xc * xc, axis=-1, keepdims=True)
    return xc * lax.rsqrt(var + LN_EPS) * g + b


def _rope_half_roll(x, cos, sin_signed):
    return x * cos + pltpu.roll(x, LANES // 2, 1) * sin_signed


def _ada_kernel(c_ref, w_ref, b_ref, o_ref):
    sc = _silu(c_ref[...]).astype(BF16)
    o_ref[...] = _dot(sc, w_ref[...].astype(BF16)) + b_ref[...]


def _ada(c_rows, w_ada, b_ada):
    rows, d = c_rows.shape
    n = w_ada.shape[1]
    tn = 1536
    return pl.pallas_call(
        _ada_kernel,
        grid=(n // tn,),
        in_specs=[pl.BlockSpec((rows, d), lambda j: (0, 0)),
                  pl.BlockSpec((d, tn), lambda j: (0, j)),
                  pl.BlockSpec((1, tn), lambda j: (0, j))],
        out_specs=pl.BlockSpec((rows, tn), lambda j: (0, j)),
        out_shape=jax.ShapeDtypeStruct((rows, n), F32),
        compiler_params=pltpu.CompilerParams(dimension_semantics=("arbitrary",), vmem_limit_bytes=VMEM_LIMIT),
        name="ada",
    )(c_rows, w_ada, b_ada.reshape(1, n))


def _in_proj_kernel(*refs, is_ctx, q_scale):
    if is_ctx:
        (x_ref, mod_ref, wa_ref, wkr_ref, wg_ref, kvn_ref, wuk_ref, wuvt_ref,
         rkt_ref, rv_ref, km_ref, vmt_ref) = refs
    else:
        (x_ref, mod_ref, wa_ref, wkr_ref, wg_ref, qn_ref, wuqt_ref, kvn_ref, wuk_ref, wuvt_ref,
         cr_ref, sr_ref, crt_ref, srt_ref, cm_ref, sma_ref, smb_ref, tqt_ref,
         rq_ref, rkt_ref, rv_ref, srg_ref, sgr_ref, sgm_ref, qmt_ref, km_ref, vmt_ref) = refs

    x = x_ref[0]
    sh1 = mod_ref[0, 0:1, :]
    s1 = mod_ref[0, 1:2, :]
    h = (x * (1.0 + s1) + sh1).astype(BF16)

    def proj(name):
        group, lo, hi = _COLS[name]
        return _dot(h, (wa_ref, wkr_ref, wg_ref)[group][:, lo:hi])

    def rms(v, g):
        return v * lax.rsqrt(jnp.mean(v * v, axis=-1, keepdims=True) + RMS_EPS) * g

    k_scale = RET_QK ** -0.5
    dkv_raw = proj("dkv")
    if is_ctx:
        kr = proj("kr")
    else:
        dq_kr = _dot(h, wkr_ref[...])
        dq_raw, kr = dq_kr[:, :MLA_Q_RANK], dq_kr[:, MLA_Q_RANK:]
    rkt = proj("rk").T
    if not is_ctx:
        rq = proj("rq")

    dkv = rms(dkv_raw, kvn_ref[...]).astype(BF16)
    kv = _dot(dkv, wuk_ref[...])
    vmt_ref[0] = _dot_nt(wuvt_ref[...], dkv).astype(BF16)
    if not is_ctx:
        dq = rms(dq_raw, qn_ref[...]).astype(BF16)
        qt = _dot_nt(wuqt_ref[...], dq)
        srg_ref[0] = _silu(proj("rg")).astype(BF16)
        sgr_ref[0] = _sigmoid(proj("gr")).astype(BF16)
        sgm_ref[0] = _sigmoid(proj("gm")).astype(BF16)
    rv_ref[0] = proj("rv").astype(BF16)

    if is_ctx:
        rkt_ref[0] = (rkt * k_scale).astype(BF16)
    else:
        cr = cr_ref[...]
        sr = sr_ref[...]
        crt = crt_ref[...]
        srt = srt_ref[...]
        half = RET_QK // 2
        for hh in range(RET_HEADS):
            sl = slice(hh * RET_QK, (hh + 1) * RET_QK)
            rq_ref[0, :, sl] = _rope_half_roll(rq[:, sl], cr, sr).astype(BF16)
            x1 = rkt[hh * RET_QK:hh * RET_QK + half]
            x2 = rkt[hh * RET_QK + half:(hh + 1) * RET_QK]
            rkt_ref[0, hh * RET_QK:hh * RET_QK + half, :] = ((x1 * crt - x2 * srt) * k_scale).astype(BF16)
            rkt_ref[0, hh * RET_QK + half:(hh + 1) * RET_QK, :] = ((x2 * crt + x1 * srt) * k_scale).astype(BF16)
        kr = (kr * cm_ref[...] + pltpu.roll(kr, LANES - _RH, 1) * sma_ref[...]
              + pltpu.roll(kr, _RH, 1) * smb_ref[...])
        tqt = tqt_ref[...]
        cos_ax = (tqt[0:_RH], tqt[2 * _RH:3 * _RH])
        sin_ax = (tqt[_RH:2 * _RH], tqt[3 * _RH:4 * _RH])
    nope_lane = lax.broadcasted_iota(jnp.int32, kr.shape, 1) < MLA_NOPE
    for hh in range(MLA_HEADS):
        sl = slice(hh * MLA_HEAD_PAD, (hh + 1) * MLA_HEAD_PAD)
        pair = kv[:, (hh // 2) * LANES:(hh // 2 + 1) * LANES]
        if hh % 2:
            pair = pltpu.roll(pair, LANES - MLA_NOPE, 1)
        km_ref[0, :, sl] = jnp.where(nope_lane, pair, kr).astype(BF16)
        if not is_ctx:
            q0 = hh * (MLA_NOPE + MLA_ROPE)
            slabs = [qt[q0:q0 + MLA_NOPE]]
            for ax in range(2):
                lo = q0 + MLA_NOPE + 2 * ax * _RH
                x1, x2 = qt[lo:lo + _RH], qt[lo + _RH:lo + 2 * _RH]
                slabs += [x1 * cos_ax[ax] - x2 * sin_ax[ax], x2 * cos_ax[ax] + x1 * sin_ax[ax]]
            slabs.append(jnp.zeros((MLA_HEAD_PAD - MLA_NOPE - MLA_ROPE, qt.shape[1]), F32))
            qmt_ref[0, sl, :] = (jnp.concatenate(slabs, axis=0) * q_scale).astype(BF16)


def _in_proj(x, mod, w_groups, qn, wuq_p, kvn, wuk_p, wuvt, tables, *, is_ctx, tm):
    B, L, D = x.shape
    w_specs = [pl.BlockSpec(w.shape, lambda b, i: (0, 0), pipeline_mode=pl.Buffered(1)) for w in w_groups]
    q_scale = (MLA_NOPE + MLA_ROPE) ** -0.5 * math.log2(math.e)
    kw = MLA_HEADS * MLA_HEAD_PAD
    vw = MLA_HEADS * MLA_V

    def const(shape):
        return pl.BlockSpec(shape, lambda b, i: (0,) * len(shape), pipeline_mode=pl.Buffered(1))

    def rows(width):
        return pl.BlockSpec((1, tm, width), lambda b, i: (b, i, 0))

    def out(width):
        return jax.ShapeDtypeStruct((B, L, width), BF16)

    x_spec = rows(D)
    def cols(height):
        return pl.BlockSpec((1, height, tm), lambda b, i: (b, 0, i))

    def out_t(height):
        return jax.ShapeDtypeStruct((B, height, L), BF16)

    if is_ctx:
        mod_spec = pl.BlockSpec((1, 6, D), lambda b, i: (B, 0, 0))
        in_specs = [x_spec, mod_spec, *w_specs, const(kvn.shape), const(wuk_p.shape), const(wuvt.shape)]
        args = (x, mod, *w_groups, kvn, wuk_p, wuvt)
        out_specs = [cols(_RQ), rows(_RV), rows(kw), cols(vw)]
        out_shape = [out_t(_RQ), out(_RV), out(kw), out_t(vw)]
    else:
        mod_spec = pl.BlockSpec((1, 6, D), lambda b, i: (b, 0, 0))
        tab = pl.BlockSpec((tm, LANES), lambda b, i: (i, 0))
        tab_t = pl.BlockSpec((RET_QK // 2, tm), lambda b, i: (0, i))
        tab_qt = pl.BlockSpec((MLA_ROPE, tm), lambda b, i: (0, i))
        in_specs = [x_spec, mod_spec, *w_specs, const(qn.shape), const(wuq_p.shape),
                    const(kvn.shape), const(wuk_p.shape), const(wuvt.shape),
                    tab, tab, tab_t, tab_t, tab, tab, tab, tab_qt]
        args = (x, mod, *w_groups, qn, wuq_p, kvn, wuk_p, wuvt) + tuple(tables)
        out_specs = [rows(_RQ), cols(_RQ), rows(_RV), rows(_RV), rows(D), rows(D), cols(kw), rows(kw), cols(vw)]
        out_shape = [out(_RQ), out_t(_RQ), out(_RV), out(_RV), out(D), out(D), out_t(kw), out(kw), out_t(vw)]
    return pl.pallas_call(
        functools.partial(_in_proj_kernel, is_ctx=is_ctx, q_scale=q_scale),
        grid=(B, L // tm),
        in_specs=in_specs,
        out_specs=out_specs,
        out_shape=out_shape,
        compiler_params=pltpu.CompilerParams(dimension_semantics=("arbitrary", "arbitrary"),
                                             vmem_limit_bytes=VMEM_LIMIT),
        name="in_proj_ctx" if is_ctx else "in_proj",
    )(*args)


def _log_sigmoid(x):
    return jnp.minimum(x, 0.0) - jnp.log(1.0 + jnp.exp(-jnp.abs(x)))


RET_LOOKAHEAD = 2


def _retention_kernel(dec_ref, q_ref, kt_ref, v_ref, kct_ref, vc_ref, o_ref, u_ref, st_ref, *, n_lat, n_ctx):
    C = RET_CHUNK
    lgf = _log_sigmoid(dec_ref[0, 0:1, :])
    lgb = _log_sigmoid(dec_ref[0, 1:2, :])
    ii = lax.broadcasted_iota(jnp.int32, (C, C), 0).astype(F32)
    jj = lax.broadcasted_iota(jnp.int32, (C, C), 1).astype(F32)
    rel = ii - jj
    mask = jnp.where(rel > 0, jnp.exp(rel * lgf), jnp.where(rel < 0, jnp.exp(-rel * lgb), 2.0))
    pos_col = ii[:, 0:1]
    pos_row = jj[0:1, :]
    lgf1 = lgf[:, 0:1]
    lgb1 = lgb[:, 0:1]
    qdec_f = jnp.exp((pos_col + 1.0) * lgf1)
    qdec_b = jnp.exp((C - pos_col) * lgb1)
    kdec_f = jnp.exp((C - 1.0 - pos_row) * lgf)
    kdec_b = jnp.exp(pos_row * lgb)
    cdec_f = jnp.exp(C * lgf1)
    cdec_b = jnp.exp(C * lgb1)

    def increment(ktref, vref, n):
        kt = ktref[0, :, n * C:(n + 1) * C].astype(F32)
        lhs = jnp.concatenate([(kt * kdec_f).astype(BF16), (kt * kdec_b).astype(BF16)], axis=0)
        return _dot(lhs, vref[0, n * C:(n + 1) * C, :])

    zero = jnp.zeros((RET_QK, RET_V), F32)
    s_f, s_b = zero, zero
    u_ctx = [increment(kct_ref, vc_ref, n) for n in range(n_ctx)]
    for n in range(n_ctx):
        s_f = s_f * cdec_f + u_ctx[n][0:RET_QK]
        s_b = s_b * cdec_b + u_ctx[n_ctx - 1 - n][RET_QK:]

    for n in range(n_lat):
        u_ref[n] = increment(kt_ref, v_ref, n)

    for t in range(n_lat):
        nb = n_lat - 1 - t
        st_ref[t, 0:RET_QK, :] = s_f.astype(BF16)
        st_ref[nb, RET_QK:, :] = s_b.astype(BF16)
        s_f = s_f * cdec_f + u_ref[t, 0:RET_QK, :]
        s_b = s_b * cdec_b + u_ref[nb, RET_QK:, :]

    scores = {}

    def score(n):
        scores[n] = _dot(q_ref[0, n * C:(n + 1) * C, :], kt_ref[0, :, n * C:(n + 1) * C])

    def consume(n):
        rows = slice(n * C, (n + 1) * C)
        a = (scores.pop(n) * mask).astype(BF16)
        q = q_ref[0, rows, :].astype(F32)
        q_dec = jnp.concatenate([(q * qdec_f).astype(BF16), (q * qdec_b).astype(BF16)], axis=1)
        o = _dot(a, v_ref[0, rows, :]) + _dot(q_dec, st_ref[n])
        mu = jnp.mean(o, axis=-1, keepdims=True)
        oc = o - mu
        var = jnp.mean(oc * oc, axis=-1, keepdims=True)
        o_ref[0, rows, :] = (oc * lax.rsqrt(var + LN_EPS)).astype(BF16)

    for n in range(n_lat + RET_LOOKAHEAD):
        if n < n_lat:
            score(n)
        if n >= RET_LOOKAHEAD:
            consume(n - RET_LOOKAHEAD)


def _retention(dec, rq, rkt, rv, rkt_c, rv_c):
    B, L, _ = rq.shape
    Lc = rv_c.shape[1]
    n_lat, n_ctx = L // RET_CHUNK, Lc // RET_CHUNK
    return pl.pallas_call(
        functools.partial(_retention_kernel, n_lat=n_lat, n_ctx=n_ctx),
        grid=(B, RET_HEADS),
        in_specs=[pl.BlockSpec((1, 2, LANES), lambda b, h: (h, 0, 0)),
                  pl.BlockSpec((1, L, RET_QK), lambda b, h: (b, 0, h)),
                  pl.BlockSpec((1, RET_QK, L), lambda b, h: (b, h, 0)),
                  pl.BlockSpec((1, L, RET_V), lambda b, h: (b, 0, h)),
                  pl.BlockSpec((1, RET_QK, Lc), lambda b, h: (b, h, 0)),
                  pl.BlockSpec((1, Lc, RET_V), lambda b, h: (b, 0, h))],
        out_specs=pl.BlockSpec((1, L, RET_V), lambda b, h: (b, 0, h)),
        out_shape=jax.ShapeDtypeStruct((B, L, RET_HEADS * RET_V), BF16),
        scratch_shapes=[pltpu.VMEM((n_lat, 2 * RET_QK, RET_V), F32),
                        pltpu.VMEM((n_lat, 2 * RET_QK, RET_V), BF16)],
        compiler_params=pltpu.CompilerParams(dimension_semantics=("arbitrary", "arbitrary"),
                                             vmem_limit_bytes=VMEM_LIMIT),
        name="retention",
    )(dec, rq, rkt, rv, rkt_c, rv_c)


ATT_SCORE_KEYS = 512
ATT_KEY_TILE = 256
ATT_Q_COLS = 256
ATT_ONES_ROWS = 16
ATT_MAX_ROWS = 64
ATT_LOOKAHEAD = 6


def _col_max(s):
    part = s[0:ATT_MAX_ROWS]
    for r in range(ATT_MAX_ROWS, s.shape[0], ATT_MAX_ROWS):
        part = jnp.maximum(part, s[r:r + ATT_MAX_ROWS])
    return jnp.max(part, axis=0, keepdims=True)


def _attention_kernel(q_ref, k_ref, vt_ref, kc_ref, vtc_ref, o_ref, *, l_lat, l_ctx):
    groups = [(kc_ref, vtc_ref, lo, min(ATT_SCORE_KEYS, l_ctx - lo)) for lo in range(0, l_ctx, ATT_SCORE_KEYS)]
    groups += [(k_ref, vt_ref, lo, ATT_SCORE_KEYS) for lo in range(0, l_lat, ATT_SCORE_KEYS)]
    tq = q_ref.shape[2]
    streams = [(hh, qb) for hh in range(2) for qb in range(tq // ATT_Q_COLS)]
    hs = [slice(hh * MLA_HEAD_PAD, (hh + 1) * MLA_HEAD_PAD) for hh in range(2)]
    vs = [slice(hh * MLA_V, (hh + 1) * MLA_V) for hh in range(2)]
    qs = [slice(qb * ATT_Q_COLS, (qb + 1) * ATT_Q_COLS) for qb in range(tq // ATT_Q_COLS)]
    m = {st: None for st in streams}
    acc = {st: None for st in streams}
    items = [(g, st, off, min(ATT_KEY_TILE, grp[3] - off))
             for g, grp in enumerate(groups) for st in streams for off in range(0, grp[3], ATT_KEY_TILE)]
    scores = {}

    def score(g, st):
        kref, _, lo, n = groups[g]
        hh, qb = st
        scores[g, st] = _dot(kref[0, lo:lo + n, hs[hh]], q_ref[0, hs[hh], qs[qb]])

    def consume(g, st, off, n):
        _, vref, lo, _ = groups[g]
        hh, _ = st
        s = scores[g, st][off:off + n]
        t_max = _col_max(s)
        m_new = t_max if m[st] is None else jnp.maximum(m[st], t_max)
        p = jnp.exp2(s - m_new).astype(BF16)
        ones = jnp.ones((ATT_ONES_ROWS, n), BF16)
        part = _dot(jnp.concatenate([vref[0, vs[hh], lo + off:lo + off + n], ones], axis=0), p)
        acc[st] = part if m[st] is None else acc[st] * jnp.exp2(m[st] - m_new) + part
        m[st] = m_new

    for i in range(len(items) + ATT_LOOKAHEAD):
        if i < len(items) and items[i][2] == 0:
            score(items[i][0], items[i][1])
        if i >= ATT_LOOKAHEAD:
            consume(*items[i - ATT_LOOKAHEAD])
    for hh, qb in streams:
        a = acc[hh, qb]
        o_ref[0, vs[hh], qs[qb]] = (a[0:MLA_V] / a[MLA_V:MLA_V + 1]).astype(BF16)


def _attention(qmt, km, vmt, km_c, vmt_c, *, tq):
    B, L, _ = km.shape
    Lc = km_c.shape[1]
    assert L % ATT_SCORE_KEYS == 0 and ATT_SCORE_KEYS % ATT_KEY_TILE == 0 and Lc % LANES == 0
    pairs = MLA_HEADS // 2
    qk_w = 2 * MLA_HEAD_PAD
    v_w = 2 * MLA_V
    return pl.pallas_call(
        functools.partial(_attention_kernel, l_lat=L, l_ctx=Lc),
        grid=(B, pairs, L // tq),
        in_specs=[pl.BlockSpec((1, qk_w, tq), lambda b, p, i: (b, p, i)),
                  pl.BlockSpec((1, L, qk_w), lambda b, p, i: (b, 0, p)),
                  pl.BlockSpec((1, v_w, L), lambda b, p, i: (b, p, 0)),
                  pl.BlockSpec((1, Lc, qk_w), lambda b, p, i: (b, 0, p)),
                  pl.BlockSpec((1, v_w, Lc), lambda b, p, i: (b, p, 0))],
        out_specs=pl.BlockSpec((1, v_w, tq), lambda b, p, i: (b, p, i)),
        out_shape=jax.ShapeDtypeStruct((B, MLA_HEADS * MLA_V, L), BF16),
        compiler_params=pltpu.CompilerParams(dimension_semantics=("arbitrary", "arbitrary", "arbitrary"),
                                             vmem_limit_bytes=VMEM_LIMIT),
        name="attention",
    )(qmt, km, vmt, km_c, vmt_c)


ROW_SUBTILE = 256


def _row_subtiles(rows):
    return [slice(lo, lo + ROW_SUBTILE) for lo in range(0, rows, ROW_SUBTILE)]


def _merge_kernel(x_ref, mod_ref, o_ret_ref, srg_ref, att_ref, sgr_ref, sgm_ref,
                  wro_ref, wmo_ref, wout_ref, g_ref, b_ref, y_ref):
    g1 = mod_ref[0, 2:3, :]
    subs = _row_subtiles(x_ref.shape[1])
    branch = []
    for r in subs:
        ret_in = srg_ref[0, r, :] * o_ret_ref[0, r, :]
        ret = _dot(ret_in, wro_ref[...])
        mla = _dot_tn(att_ref[0, :, r], wmo_ref[...])
        branch.append((ret, mla))
    for r, (ret, mla) in zip(subs, branch):
        mix = (sgr_ref[0, r, :].astype(F32) * ret + sgm_ref[0, r, :].astype(F32) * mla).astype(BF16)
        y = _dot(mix, wout_ref[...])
        y_ref[0, r, :] = _layer_norm(ALPHA * x_ref[0, r, :] + g1 * y, g_ref[...], b_ref[...])


def _merge(x, mod, o_ret, srg, att, sgr, sgm, wro, wmo, wout, ln_g, ln_b, *, tm):
    B, L, D = x.shape

    def const(shape):
        return pl.BlockSpec(shape, lambda b, i: (0,) * len(shape), pipeline_mode=pl.Buffered(1))

    def rows(width):
        return pl.BlockSpec((1, tm, width), lambda b, i: (b, i, 0))

    return pl.pallas_call(
        _merge_kernel,
        grid=(B, L // tm),
        in_specs=[rows(D), pl.BlockSpec((1, 6, D), lambda b, i: (b, 0, 0)),
                  rows(o_ret.shape[2]), rows(srg.shape[2]),
                  pl.BlockSpec((1, att.shape[1], tm), lambda b, i: (b, 0, i)), rows(D), rows(D),
                  const(wro.shape), const(wmo.shape), const(wout.shape), const(ln_g.shape), const(ln_b.shape)],
        out_specs=rows(D),
        out_shape=jax.ShapeDtypeStruct((B, L, D), F32),
        compiler_params=pltpu.CompilerParams(dimension_semantics=("arbitrary", "arbitrary"),
                                             vmem_limit_bytes=VMEM_LIMIT),
        name="merge",
    )(x, mod, o_ret, srg, att, sgr, sgm, wro, wmo, wout, ln_g, ln_b)


def _ffn_kernel(x_ref, mod_ref, wgu_ref, wdn_ref, g_ref, b_ref, y_ref):
    sh2 = mod_ref[0, 3:4, :]
    s2 = mod_ref[0, 4:5, :]
    g2 = mod_ref[0, 5:6, :]
    subs = _row_subtiles(x_ref.shape[1])
    gate_up = []
    for r in subs:
        h = (x_ref[0, r, :] * (1.0 + s2) + sh2).astype(BF16)
        gate_up.append((_dot(h, wgu_ref[:, 0:D_FF]), _dot(h, wgu_ref[:, D_FF:2 * D_FF])))
    fs = [_dot((_silu(a) * b).astype(BF16), wdn_ref[...]) for a, b in gate_up]
    for r, f in zip(subs, fs):
        y_ref[0, r, :] = _layer_norm(ALPHA * x_ref[0, r, :] + g2 * f, g_ref[...], b_ref[...])


def _ffn(x, mod, wgu, wdn, ln_g, ln_b, *, tm):
    B, L, D = x.shape

    def const(shape):
        return pl.BlockSpec(shape, lambda b, i: (0,) * len(shape), pipeline_mode=pl.Buffered(1))

    rows = pl.BlockSpec((1, tm, D), lambda b, i: (b, i, 0))
    return pl.pallas_call(
        _ffn_kernel,
        grid=(B, L // tm),
        in_specs=[rows, pl.BlockSpec((1, 6, D), lambda b, i: (b, 0, 0)),
                  const(wgu.shape), const(wdn.shape), const(ln_g.shape), const(ln_b.shape)],
        out_specs=rows,
        out_shape=jax.ShapeDtypeStruct((B, L, D), F32),
        compiler_params=pltpu.CompilerParams(dimension_semantics=("arbitrary", "arbitrary"),
                                             vmem_limit_bytes=VMEM_LIMIT),
        name="ffn",
    )(x, mod, wgu, wdn, ln_g, ln_b)


def _rope_tables(L):
    t = np.arange(L, dtype=np.float64)
    half = RET_QK // 2
    fr = RET_ROPE_BASE ** (-np.arange(half, dtype=np.float64) / half)
    ang = t[:, None] * fr[None, :]
    cos_r = np.concatenate([np.cos(ang), np.cos(ang)], -1)
    sin_r = np.concatenate([-np.sin(ang), np.sin(ang)], -1)

    row = np.floor(t / GRID_W)
    col = t - row * GRID_W
    fm = ROPE_BASE ** (-np.arange(_RH, dtype=np.float64) / _RH)
    cos_t = np.ascontiguousarray(np.cos(ang).T)
    sin_t = np.ascontiguousarray(np.sin(ang).T)

    ang_ax = (row[:, None] * fm[None, :], col[:, None] * fm[None, :])
    cos_k = np.ones((L, LANES))
    sin_up = np.zeros((L, LANES))
    sin_down = np.zeros((L, LANES))
    for ax in range(2):
        lo = MLA_NOPE + 2 * ax * _RH
        cos_k[:, lo:lo + 2 * _RH] = np.concatenate([np.cos(ang_ax[ax])] * 2, -1)
        sin_up[:, lo:lo + _RH] = -np.sin(ang_ax[ax])
        sin_down[:, lo + _RH:lo + 2 * _RH] = np.sin(ang_ax[ax])
    tab_qt = np.ascontiguousarray(np.concatenate(
        [np.cos(ang_ax[0]), np.sin(ang_ax[0]), np.cos(ang_ax[1]), np.sin(ang_ax[1])], -1).T)
    return tuple(jnp.asarray(a, F32) for a in (cos_r, sin_r, cos_t, sin_t, cos_k, sin_up, sin_down, tab_qt))


def _prep_weights(w_in, w_uq, w_ukv):
    hq, hkv, hp = MLA_NOPE + MLA_ROPE, MLA_NOPE + MLA_V, MLA_HEAD_PAD
    w_a = w_in[:, :_KR_OFF].astype(BF16)
    dq_lo = _KR_OFF - MLA_KV_RANK - MLA_Q_RANK
    w_kr = jnp.concatenate(
        [w_in[:, dq_lo:dq_lo + MLA_Q_RANK],
         jnp.pad(w_in[:, _KR_OFF:_KR_OFF + MLA_ROPE], ((0, 0), (MLA_NOPE, hp - hq)))], axis=1).astype(BF16)
    w_g = w_in[:, _KR_OFF + MLA_ROPE:].astype(BF16)
    wuq_pt = w_uq.astype(BF16).T
    kv3 = w_ukv.reshape(w_ukv.shape[0], MLA_HEADS, hkv)
    wuk_p = kv3[:, :, :MLA_NOPE].reshape(-1, MLA_HEADS * MLA_NOPE).astype(BF16)
    wuvt = kv3[:, :, MLA_NOPE:].reshape(-1, MLA_HEADS * MLA_V).astype(BF16).T
    return (w_a, w_kr, w_g), wuq_pt, wuk_p, wuvt


def kernel(x, c, ctx, c_ctx, w_ada, b_ada, w_in, ret_decay_f, ret_decay_b, w_ret_o, mla_q_norm, w_uq,
           mla_kv_norm, w_ukv, w_mla_o, w_out, ln1_g, ln1_b, w_gu, w_down, ln2_g, ln2_b):
    B, L, D = x.shape
    assert w_ada.shape[0] == DEPTH == 1
    i = 0

    c_rows = jnp.concatenate([c, c_ctx[None, :], jnp.zeros((8 - B - 1, D), F32)], axis=0)
    mod = _ada(c_rows, w_ada[i], b_ada[i]).reshape(8, 6, D)
    mod_l = mod_c = mod

    w_groups, wuq_p, wuk_p, wuvt = _prep_weights(w_in[i], w_uq[i], w_ukv[i])
    qn = mla_q_norm[i].reshape(1, -1)
    kvn = mla_kv_norm[i].reshape(1, -1)
    tables = _rope_tables(L)

    rq, rk, rv, srg, sgr, sgm, qm, km, vm = _in_proj(
        x, mod_l, w_groups, qn, wuq_p, kvn, wuk_p, wuvt, tables, is_ctx=False, tm=256)
    rk_c, rv_c, km_c, vm_c = _in_proj(
        ctx, mod_c, w_groups, None, None, kvn, wuk_p, wuvt, None, is_ctx=True, tm=256)

    dec = jnp.stack([jnp.broadcast_to(ret_decay_f[i][:, None], (RET_HEADS, LANES)),
                     jnp.broadcast_to(ret_decay_b[i][:, None], (RET_HEADS, LANES))], axis=1)
    o_ret = _retention(dec, rq, rk, rv, rk_c, rv_c)
    att = _attention(qm, km, vm, km_c, vm_c, tq=1024)

    x1 = _merge(x, mod_l, o_ret, srg, att, sgr, sgm,
                w_ret_o[i].astype(BF16), w_mla_o[i].astype(BF16), w_out[i].astype(BF16),
                ln1_g[i].reshape(1, D), ln1_b[i].reshape(1, D), tm=512)
    return _ffn(x1, mod_l, w_gu[i].astype(BF16), w_down[i].astype(BF16),
                ln2_g[i].reshape(1, D), ln2_b[i].reshape(1, D), tm=512)
```

```python
import functools
import math

import numpy as np
import jax
import jax.numpy as jnp
from jax import lax
from jax.experimental import pallas as pl
from jax.experimental.pallas import tpu as pltpu

F32 = jnp.float32
BF16 = jnp.bfloat16

D_MODEL = 1024
GRID_W = 64
RET_HEADS = 4
RET_QK = 128
RET_V = 256
RET_CHUNK = 128
RET_ROPE_BASE = 10000.0
MLA_HEADS = 8
MLA_NOPE = 64
MLA_ROPE = 32
MLA_V = 64
MLA_Q_RANK = 384
MLA_KV_RANK = 256
ROPE_BASE = 10000.0
D_FF = -(-8 * D_MODEL // (3 * 256)) * 256
LN_EPS = 1e-5
RMS_EPS = 1e-6
DEPTH = 1
ALPHA = (2.0 * DEPTH) ** 0.25

LANES = 128
MLA_HEAD_PAD = LANES
VMEM_LIMIT = 56 * 1024 * 1024

_RQ = RET_HEADS * RET_QK
_RV = RET_HEADS * RET_V
_KR_OFF = 2 * _RQ + 2 * _RV + MLA_Q_RANK + MLA_KV_RANK
_COLS = {
    "rq": (0, 0, _RQ),
    "rk": (0, _RQ, 2 * _RQ),
    "rv": (0, 2 * _RQ, 2 * _RQ + _RV),
    "rg": (0, 2 * _RQ + _RV, 2 * _RQ + 2 * _RV),
    "dkv": (0, 2 * _RQ + 2 * _RV + MLA_Q_RANK, _KR_OFF),
    "dq": (1, 0, MLA_Q_RANK),
    "kr": (1, MLA_Q_RANK, MLA_Q_RANK + LANES),
    "gr": (2, 0, D_MODEL),
    "gm": (2, D_MODEL, 2 * D_MODEL),
}

_RH = MLA_ROPE // 4


def _dot(a, b):
    return jnp.dot(a, b, preferred_element_type=F32)


def _dot_nt(a, b):
    return lax.dot_general(a, b, (((1,), (1,)), ((), ())), preferred_element_type=F32)


def _dot_tn(a, b):
    return lax.dot_general(a, b, (((0,), (0,)), ((), ())), preferred_element_type=F32)


def _sigmoid(x):
    return 1.0 / (1.0 + jnp.exp(-x))


def _silu(x):
    return x * _sigmoid(x)


def _layer_norm(x, g, b):
    mu = jnp.mean(x, axis=-1, keepdims=True)
    xc = x - mu
    var = jnp.mean(xc * xc, axis=-1, keepdims=True)
    return xc * lax.rsqrt(var + LN_EPS) * g + b


def _rope_half_roll(x, cos, sin_signed):
    return x * cos + pltpu.roll(x, LANES // 2, 1) * sin_signed


def _ada_kernel(c_ref, w_ref, b_ref, o_ref):
    sc = _silu(c_ref[...]).astype(BF16)
    o_ref[...] = _dot(sc, w_ref[...].astype(BF16)) + b_ref[...]


def _ada(c_rows, w_ada, b_ada):
    rows, d = c_rows.shape
    n = w_ada.shape[1]
    tn = 1536
    return pl.pallas_call(
        _ada_kernel,
        grid=(n // tn,),
        in_specs=[pl.BlockSpec((rows, d), lambda j: (0, 0)),
                  pl.BlockSpec((d, tn), lambda j: (0, j)),
                  pl.BlockSpec((1, tn), lambda j: (0, j))],
        out_specs=pl.BlockSpec((rows, tn), lambda j: (0, j)),
        out_shape=jax.ShapeDtypeStruct((rows, n), F32),
        compiler_params=pltpu.CompilerParams(dimension_semantics=("arbitrary",), vmem_limit_bytes=VMEM_LIMIT),
        name="ada",
    )(c_rows, w_ada, b_ada.reshape(1, n))


def _in_proj_kernel(*refs, is_ctx, q_scale):
    if is_ctx:
        (x_ref, mod_ref, wa_ref, wkr_ref, wg_ref, kvn_ref, wuk_ref, wuvt_ref,
         rkt_ref, rv_ref, km_ref, vmt_ref) = refs
    else:
        (x_ref, mod_ref, wa_ref, wkr_ref, wg_ref, qn_ref, wuqt_ref, kvn_ref, wuk_ref, wuvt_ref,
         cr_ref, sr_ref, crt_ref, srt_ref, cm_ref, sma_ref, smb_ref, tqt_ref,
         rq_ref, rkt_ref, rv_ref, srg_ref, sgr_ref, sgm_ref, qmt_ref, km_ref, vmt_ref) = refs

    x = x_ref[0]
    sh1 = mod_ref[0, 0:1, :]
    s1 = mod_ref[0, 1:2, :]
    h = (x * (1.0 + s1) + sh1).astype(BF16)

    def proj(name):
        group, lo, hi = _COLS[name]
        return _dot(h, (wa_ref, wkr_ref, wg_ref)[group][:, lo:hi])

    def rms(v, g):
        return v * lax.rsqrt(jnp.mean(v * v, axis=-1, keepdims=True) + RMS_EPS) * g

    k_scale = RET_QK ** -0.5
    dkv_raw = proj("dkv")
    if is_ctx:
        kr = proj("kr")
    else:
        dq_kr = _dot(h, wkr_ref[...])
        dq_raw, kr = dq_kr[:, :MLA_Q_RANK], dq_kr[:, MLA_Q_RANK:]
    rkt = proj("rk").T
    if not is_ctx:
        rq = proj("rq")

    dkv = rms(dkv_raw, kvn_ref[...]).astype(BF16)
    kv = _dot(dkv, wuk_ref[...])
    vmt_ref[0] = _dot_nt(wuvt_ref[...], dkv).astype(BF16)
    if not is_ctx:
        dq = rms(dq_raw, qn_ref[...]).astype(BF16)
        qt = _dot_nt(wuqt_ref[...], dq)
        srg_ref[0] = _silu(proj("rg")).astype(BF16)
        sgr_ref[0] = _sigmoid(proj("gr")).astype(BF16)
        sgm_ref[0] = _sigmoid(proj("gm")).astype(BF16)
    rv_ref[0] = proj("rv").astype(BF16)

    if is_ctx:
        rkt_ref[0] = (rkt * k_scale).astype(BF16)
    else:
        cr = cr_ref[...]
        sr = sr_ref[...]
        crt = crt_ref[...]
        srt = srt_ref[...]
        half = RET_QK // 2
        for hh in range(RET_HEADS):
            sl = slice(hh * RET_QK, (hh + 1) * RET_QK)
            rq_ref[0, :, sl] = _rope_half_roll(rq[:, sl], cr, sr).astype(BF16)
            x1 = rkt[hh * RET_QK:hh * RET_QK + half]
            x2 = rkt[hh * RET_QK + half:(hh + 1) * RET_QK]
            rkt_ref[0, hh * RET_QK:hh * RET_QK + half, :] = ((x1 * crt - x2 * srt) * k_scale).astype(BF16)
            rkt_ref[0, hh * RET_QK + half:(hh + 1) * RET_QK, :] = ((x2 * crt + x1 * srt) * k_scale).astype(BF16)
        kr = (kr * cm_ref[...] + pltpu.roll(kr, LANES - _RH, 1) * sma_ref[...]
              + pltpu.roll(kr, _RH, 1) * smb_ref[...])
        tqt = tqt_ref[...]
        cos_ax = (tqt[0:_RH], tqt[2 * _RH:3 * _RH])
        sin_ax = (tqt[_RH:2 * _RH], tqt[3 * _RH:4 * _RH])
    nope_lane = lax.broadcasted_iota(jnp.int32, kr.shape, 1) < MLA_NOPE
    for hh in range(MLA_HEADS):
        sl = slice(hh * MLA_HEAD_PAD, (hh + 1) * MLA_HEAD_PAD)
        pair = kv[:, (hh // 2) * LANES:(hh // 2 + 1) * LANES]
        if hh % 2:
            pair = pltpu.roll(pair, LANES - MLA_NOPE, 1)
        km_ref[0, :, sl] = jnp.where(nope_lane, pair, kr).astype(BF16)
        if not is_ctx:
            q0 = hh * (MLA_NOPE + MLA_ROPE)
            slabs = [qt[q0:q0 + MLA_NOPE]]
            for ax in range(2):
                lo = q0 + MLA_NOPE + 2 * ax * _RH
                x1, x2 = qt[lo:lo + _RH], qt[lo + _RH:lo + 2 * _RH]
                slabs += [x1 * cos_ax[ax] - x2 * sin_ax[ax], x2 * cos_ax[ax] + x1 * sin_ax[ax]]
            slabs.append(jnp.zeros((MLA_HEAD_PAD - MLA_NOPE - MLA_ROPE, qt.shape[1]), F32))
            qmt_ref[0, sl, :] = (jnp.concatenate(slabs, axis=0) * q_scale).astype(BF16)


def _in_proj(x, mod, w_groups, qn, wuq_p, kvn, wuk_p, wuvt, tables, *, is_ctx, tm):
    B, L, D = x.shape
    w_specs = [pl.BlockSpec(w.shape, lambda b, i: (0, 0), pipeline_mode=pl.Buffered(1)) for w in w_groups]
    q_scale = (MLA_NOPE + MLA_ROPE) ** -0.5 * math.log2(math.e)
    kw = MLA_HEADS * MLA_HEAD_PAD
    vw = MLA_HEADS * MLA_V

    def const(shape):
        return pl.BlockSpec(shape, lambda b, i: (0,) * len(shape), pipeline_mode=pl.Buffered(1))

    def rows(width):
        return pl.BlockSpec((1, tm, width), lambda b, i: (b, i, 0))

    def out(width):
        return jax.ShapeDtypeStruct((B, L, width), BF16)

    x_spec = rows(D)
    def cols(height):
        return pl.BlockSpec((1, height, tm), lambda b, i: (b, 0, i))

    def out_t(height):
        return jax.ShapeDtypeStruct((B, height, L), BF16)

    if is_ctx:
        mod_spec = pl.BlockSpec((1, 6, D), lambda b, i: (B, 0, 0))
        in_specs = [x_spec, mod_spec, *w_specs, const(kvn.shape), const(wuk_p.shape), const(wuvt.shape)]
        args = (x, mod, *w_groups, kvn, wuk_p, wuvt)
        out_specs = [cols(_RQ), rows(_RV), rows(kw), cols(vw)]
        out_shape = [out_t(_RQ), out(_RV), out(kw), out_t(vw)]
    else:
        mod_spec = pl.BlockSpec((1, 6, D), lambda b, i: (b, 0, 0))
        tab = pl.BlockSpec((tm, LANES), lambda b, i: (i, 0))
        tab_t = pl.BlockSpec((RET_QK // 2, tm), lambda b, i: (0, i))
        tab_qt = pl.BlockSpec((MLA_ROPE, tm), lambda b, i: (0, i))
        in_specs = [x_spec, mod_spec, *w_specs, const(qn.shape), const(wuq_p.shape),
                    const(kvn.shape), const(wuk_p.shape), const(wuvt.shape),
                    tab, tab, tab_t, tab_t, tab, tab, tab, tab_qt]
        args = (x, mod, *w_groups, qn, wuq_p, kvn, wuk_p, wuvt) + tuple(tables)
        out_specs = [rows(_RQ), cols(_RQ), rows(_RV), rows(_RV), rows(D), rows(D), cols(kw), rows(kw), cols(vw)]
        out_shape = [out(_RQ), out_t(_RQ), out(_RV), out(_RV), out(D), out(D), out_t(kw), out(kw), out_t(vw)]
    return pl.pallas_call(
        functools.partial(_in_proj_kernel, is_ctx=is_ctx, q_scale=q_scale),
        grid=(B, L // tm),
        in_specs=in_specs,
        out_specs=out_specs,
        out_shape=out_shape,
        compiler_params=pltpu.CompilerParams(dimension_semantics=("arbitrary", "arbitrary"),
                                             vmem_limit_bytes=VMEM_LIMIT),
        name="in_proj_ctx" if is_ctx else "in_proj",
    )(*args)


def _log_sigmoid(x):
    return jnp.minimum(x, 0.0) - jnp.log(1.0 + jnp.exp(-jnp.abs(x)))


RET_LOOKAHEAD = 2


def _retention_kernel(dec_ref, q_ref, kt_ref, v_ref, kct_ref, vc_ref, o_ref, u_ref, st_ref, *, n_lat, n_ctx):
    C = RET_CHUNK
    lgf = _log_sigmoid(dec_ref[0, 0:1, :])
    lgb = _log_sigmoid(dec_ref[0, 1:2, :])
    ii = lax.broadcasted_iota(jnp.int32, (C, C), 0).astype(F32)
    jj = lax.broadcasted_iota(jnp.int32, (C, C), 1).astype(F32)
    rel = ii - jj
    mask = jnp.where(rel > 0, jnp.exp(rel * lgf), jnp.where(rel < 0, jnp.exp(-rel * lgb), 2.0))
    pos_col = ii[:, 0:1]
    pos_row = jj[0:1, :]
    lgf1 = lgf[:, 0:1]
    lgb1 = lgb[:, 0:1]
    qdec_f = jnp.exp((pos_col + 1.0) * lgf1)
    qdec_b = jnp.exp((C - pos_col) * lgb1)
    kdec_f = jnp.exp((C - 1.0 - pos_row) * lgf)
    kdec_b = jnp.exp(pos_row * lgb)
    cdec_f = jnp.exp(C * lgf1)
    cdec_b = jnp.exp(C * lgb1)

    def increment(ktref, vref, n):
        kt = ktref[0, :, n * C:(n + 1) * C].astype(F32)
        lhs = jnp.concatenate([(kt * kdec_f).astype(BF16), (kt * kdec_b).astype(BF16)], axis=0)
        return _dot(lhs, vref[0, n * C:(n + 1) * C, :])

    zero = jnp.zeros((RET_QK, RET_V), F32)
    s_f, s_b = zero, zero
    u_ctx = [increment(kct_ref, vc_ref, n) for n in range(n_ctx)]
    for n in range(n_ctx):
        s_f = s_f * cdec_f + u_ctx[n][0:RET_QK]
        s_b = s_b * cdec_b + u_ctx[n_ctx - 1 - n][RET_QK:]

    for n in range(n_lat):
        u_ref[n] = increment(kt_ref, v_ref, n)

    for t in range(n_lat):
        nb = n_lat - 1 - t
        st_ref[t, 0:RET_QK, :] = s_f.astype(BF16)
        st_ref[nb, RET_QK:, :] = s_b.astype(BF16)
        s_f = s_f * cdec_f + u_ref[t, 0:RET_QK, :]
        s_b = s_b * cdec_b + u_ref[nb, RET_QK:, :]

    scores = {}

    def score(n):
        scores[n] = _dot(q_ref[0, n * C:(n + 1) * C, :], kt_ref[0, :, n * C:(n + 1) * C])

    def consume(n):
        rows = slice(n * C, (n + 1) * C)
        a = (scores.pop(n) * mask).astype(BF16)
        q = q_ref[0, rows, :].astype(F32)
        q_dec = jnp.concatenate([(q * qdec_f).astype(BF16), (q * qdec_b).astype(BF16)], axis=1)
        o = _dot(a, v_ref[0, rows, :]) + _dot(q_dec, st_ref[n])
        mu = jnp.mean(o, axis=-1, keepdims=True)
        oc = o - mu
        var = jnp.mean(oc * oc, axis=-1, keepdims=True)
        o_ref[0, rows, :] = (oc * lax.rsqrt(var + LN_EPS)).astype(BF16)

    for n in range(n_lat + RET_LOOKAHEAD):
        if n < n_lat:
            score(n)
        if n >= RET_LOOKAHEAD:
            consume(n - RET_LOOKAHEAD)


def _retention(dec, rq, rkt, rv, rkt_c, rv_c):
    B, L, _ = rq.shape
    Lc = rv_c.shape[1]
    n_lat, n_ctx = L // RET_CHUNK, Lc // RET_CHUNK
    return pl.pallas_call(
        functools.partial(_retention_kernel, n_lat=n_lat, n_ctx=n_ctx),
        grid=(B, RET_HEADS),
        in_specs=[pl.BlockSpec((1, 2, LANES), lambda b, h: (h, 0, 0)),
                  pl.BlockSpec((1, L, RET_QK), lambda b, h: (b, 0, h)),
                  pl.BlockSpec((1, RET_QK, L), lambda b, h: (b, h, 0)),
                  pl.BlockSpec((1, L, RET_V), lambda b, h: (b, 0, h)),
                  pl.BlockSpec((1, RET_QK, Lc), lambda b, h: (b, h, 0)),
                  pl.BlockSpec((1, Lc, RET_V), lambda b, h: (b, 0, h))],
        out_specs=pl.BlockSpec((1, L, RET_V), lambda b, h: (b, 0, h)),
        out_shape=jax.ShapeDtypeStruct((B, L, RET_HEADS * RET_V), BF16),
        scratch_shapes=[pltpu.VMEM((n_lat, 2 * RET_QK, RET_V), F32),
                        pltpu.VMEM((n_lat, 2 * RET_QK, RET_V), BF16)],
        compiler_params=pltpu.CompilerParams(dimension_semantics=("arbitrary", "arbitrary"),
                                             vmem_limit_bytes=VMEM_LIMIT),
        name="retention",
    )(dec, rq, rkt, rv, rkt_c, rv_c)


ATT_SCORE_KEYS = 512
ATT_KEY_TILE = 256
ATT_Q_COLS = 256
ATT_ONES_ROWS = 16
ATT_MAX_ROWS = 64
ATT_LOOKAHEAD = 6


def _col_max(s):
    part = s[0:ATT_MAX_ROWS]
    for r in range(ATT_MAX_ROWS, s.shape[0], ATT_MAX_ROWS):
        part = jnp.maximum(part, s[r:r + ATT_MAX_ROWS])
    return jnp.max(part, axis=0, keepdims=True)


def _attention_kernel(q_ref, k_ref, vt_ref, kc_ref, vtc_ref, o_ref, *, l_lat, l_ctx):
    groups = [(kc_ref, vtc_ref, lo, min(ATT_SCORE_KEYS, l_ctx - lo)) for lo in range(0, l_ctx, ATT_SCORE_KEYS)]
    groups += [(k_ref, vt_ref, lo, ATT_SCORE_KEYS) for lo in range(0, l_lat, ATT_SCORE_KEYS)]
    tq = q_ref.shape[2]
    streams = [(hh, qb) for hh in range(2) for qb in range(tq // ATT_Q_COLS)]
    hs = [slice(hh * MLA_HEAD_PAD, (hh + 1) * MLA_HEAD_PAD) for hh in range(2)]
    vs = [slice(hh * MLA_V, (hh + 1) * MLA_V) for hh in range(2)]
    qs = [slice(qb * ATT_Q_COLS, (qb + 1) * ATT_Q_COLS) for qb in range(tq // ATT_Q_COLS)]
    m = {st: None for st in streams}
    acc = {st: None for st in streams}
    items = [(g, st, off, min(ATT_KEY_TILE, grp[3] - off))
             for g, grp in enumerate(groups) for st in streams for off in range(0, grp[3], ATT_KEY_TILE)]
    scores = {}

    def score(g, st):
        kref, _, lo, n = groups[g]
        hh, qb = st
        scores[g, st] = _dot(kref[0, lo:lo + n, hs[hh]], q_ref[0, hs[hh], qs[qb]])

    def consume(g, st, off, n):
        _, vref, lo, _ = groups[g]
        hh, _ = st
        s = scores[g, st][off:off + n]
        t_max = _col_max(s)
        m_new = t_max if m[st] is None else jnp.maximum(m[st], t_max)
        p = jnp.exp2(s - m_new).astype(BF16)
        ones = jnp.ones((ATT_ONES_ROWS, n), BF16)
        part = _dot(jnp.concatenate([vref[0, vs[hh], lo + off:lo + off + n], ones], axis=0), p)
        acc[st] = part if m[st] is None else acc[st] * jnp.exp2(m[st] - m_new) + part
        m[st] = m_new

    for i in range(len(items) + ATT_LOOKAHEAD):
        if i < len(items) and items[i][2] == 0:
            score(items[i][0], items[i][1])
        if i >= ATT_LOOKAHEAD:
            consume(*items[i - ATT_LOOKAHEAD])
    for hh, qb in streams:
        a = acc[hh, qb]
        o_ref[0, vs[hh], qs[qb]] = (a[0:MLA_V] / a[MLA_V:MLA_V + 1]).astype(BF16)


def _attention(qmt, km, vmt, km_c, vmt_c, *, tq):
    B, L, _ = km.shape
    Lc = km_c.shape[1]
    assert L % ATT_SCORE_KEYS == 0 and ATT_SCORE_KEYS % ATT_KEY_TILE == 0 and Lc % LANES == 0
    pairs = MLA_HEADS // 2
    qk_w = 2 * MLA_HEAD_PAD
    v_w = 2 * MLA_V
    return pl.pallas_call(
        functools.partial(_attention_kernel, l_lat=L, l_ctx=Lc),
        grid=(B, pairs, L // tq),
        in_specs=[pl.BlockSpec((1, qk_w, tq), lambda b, p, i: (b, p, i)),
                  pl.BlockSpec((1, L, qk_w), lambda b, p, i: (b, 0, p)),
                  pl.BlockSpec((1, v_w, L), lambda b, p, i: (b, p, 0)),
                  pl.BlockSpec((1, Lc, qk_w), lambda b, p, i: (b, 0, p)),
                  pl.BlockSpec((1, v_w, Lc), lambda b, p, i: (b, p, 0))],
        out_specs=pl.BlockSpec((1, v_w, tq), lambda b, p, i: (b, p, i)),
        out_shape=jax.ShapeDtypeStruct((B, MLA_HEADS * MLA_V, L), BF16),
        compiler_params=pltpu.CompilerParams(dimension_semantics=("arbitrary", "arbitrary", "arbitrary"),
                                             vmem_limit_bytes=VMEM_LIMIT),
        name="attention",
    )(qmt, km, vmt, km_c, vmt_c)


ROW_SUBTILE = 256


def _row_subtiles(rows):
    return [slice(lo, lo + ROW_SUBTILE) for lo in range(0, rows, ROW_SUBTILE)]


def _merge_kernel(x_ref, mod_ref, o_ret_ref, srg_ref, att_ref, sgr_ref, sgm_ref,
                  wro_ref, wmo_ref, wout_ref, g_ref, b_ref, y_ref):
    g1 = mod_ref[0, 2:3, :]
    subs = _row_subtiles(x_ref.shape[1])
    branch = []
    for r in subs:
        ret_in = srg_ref[0, r, :] * o_ret_ref[0, r, :]
        ret = _dot(ret_in, wro_ref[...])
        mla = _dot_tn(att_ref[0, :, r], wmo_ref[...])
        branch.append((ret, mla))
    for r, (ret, mla) in zip(subs, branch):
        mix = (sgr_ref[0, r, :].astype(F32) * ret + sgm_ref[0, r, :].astype(F32) * mla).astype(BF16)
        y = _dot(mix, wout_ref[...])
        y_ref[0, r, :] = _layer_norm(ALPHA * x_ref[0, r, :] + g1 * y, g_ref[...], b_ref[...])


def _merge(x, mod, o_ret, srg, att, sgr, sgm, wro, wmo, wout, ln_g, ln_b, *, tm):
    B, L, D = x.shape

    def const(shape):
        return pl.BlockSpec(shape, lambda b, i: (0,) * len(shape), pipeline_mode=pl.Buffered(1))

    def rows(width):
        return pl.BlockSpec((1, tm, width), lambda b, i: (b, i, 0))

    return pl.pallas_call(
        _merge_kernel,
        grid=(B, L // tm),
        in_specs=[rows(D), pl.BlockSpec((1, 6, D), lambda b, i: (b, 0, 0)),
                  rows(o_ret.shape[2]), rows(srg.shape[2]),
                  pl.BlockSpec((1, att.shape[1], tm), lambda b, i: (b, 0, i)), rows(D), rows(D),
                  const(wro.shape), const(wmo.shape), const(wout.shape), const(ln_g.shape), const(ln_b.shape)],
        out_specs=rows(D),
        out_shape=jax.ShapeDtypeStruct((B, L, D), F32),
        compiler_params=pltpu.CompilerParams(dimension_semantics=("arbitrary", "arbitrary"),
                                             vmem_limit_bytes=VMEM_LIMIT),
        name="merge",
    )(x, mod, o_ret, srg, att, sgr, sgm, wro, wmo, wout, ln_g, ln_b)


def _ffn_kernel(x_ref, mod_ref, wgu_ref, wdn_ref, g_ref, b_ref, y_ref):
    sh2 = mod_ref[0, 3:4, :]
    s2 = mod_ref[0, 4:5, :]
    g2 = mod_ref[0, 5:6, :]
    subs = _row_subtiles(x_ref.shape[1])
    gate_up = []
    for r in subs:
        h = (x_ref[0, r, :] * (1.0 + s2) + sh2).astype(BF16)
        gate_up.append((_dot(h, wgu_ref[:, 0:D_FF]), _dot(h, wgu_ref[:, D_FF:2 * D_FF])))
    fs = [_dot((_silu(a) * b).astype(BF16), wdn_ref[...]) for a, b in gate_up]
    for r, f in zip(subs, fs):
        y_ref[0, r, :] = _layer_norm(ALPHA * x_ref[0, r, :] + g2 * f, g_ref[...], b_ref[...])


def _ffn(x, mod, wgu, wdn, ln_g, ln_b, *, tm):
    B, L, D = x.shape

    def const(shape):
        return pl.BlockSpec(shape, lambda b, i: (0,) * len(shape), pipeline_mode=pl.Buffered(1))

    rows = pl.BlockSpec((1, tm, D), lambda b, i: (b, i, 0))
    return pl.pallas_call(
        _ffn_kernel,
        grid=(B, L // tm),
        in_specs=[rows, pl.BlockSpec((1, 6, D), lambda b, i: (b, 0, 0)),
                  const(wgu.shape), const(wdn.shape), const(ln_g.shape), const(ln_b.shape)],
        out_specs=rows,
        out_shape=jax.ShapeDtypeStruct((B, L, D), F32),
        compiler_params=pltpu.CompilerParams(dimension_semantics=("arbitrary", "arbitrary"),
                                             vmem_limit_bytes=VMEM_LIMIT),
        name="ffn",
    )(x, mod, wgu, wdn, ln_g, ln_b)


def _rope_tables(L):
    t = np.arange(L, dtype=np.float64)
    half = RET_QK // 2
    fr = RET_ROPE_BASE ** (-np.arange(half, dtype=np.float64) / half)
    ang = t[:, None] * fr[None, :]
    cos_r = np.concatenate([np.cos(ang), np.cos(ang)], -1)
    sin_r = np.concatenate([-np.sin(ang), np.sin(ang)], -1)

    row = np.floor(t / GRID_W)
    col = t - row * GRID_W
    fm = ROPE_BASE ** (-np.arange(_RH, dtype=np.float64) / _RH)
    cos_t = np.ascontiguousarray(np.cos(ang).T)
    sin_t = np.ascontiguousarray(np.sin(ang).T)

    ang_ax = (row[:, None] * fm[None, :], col[:, None] * fm[None, :])
    cos_k = np.ones((L, LANES))
    sin_up = np.zeros((L, LANES))
    sin_down = np.zeros((L, LANES))
    for ax in range(2):
        lo = MLA_NOPE + 2 * ax * _RH
        cos_k[:, lo:lo + 2 * _RH] = np.concatenate([np.cos(ang_ax[ax])] * 2, -1)
        sin_up[:, lo:lo + _RH] = -np.sin(ang_ax[ax])
        sin_down[:, lo + _RH:lo + 2 * _RH] = np.sin(ang_ax[ax])
    tab_qt = np.ascontiguousarray(np.concatenate(
        [np.cos(ang_ax[0]), np.sin(ang_ax[0]), np.cos(ang_ax[1]), np.sin(ang_ax[1])], -1).T)
    return tuple(jnp.asarray(a, F32) for a in (cos_r, sin_r, cos_t, sin_t, cos_k, sin_up, sin_down, tab_qt))


def _prep_weights(w_in, w_uq, w_ukv):
    hq, hkv, hp = MLA_NOPE + MLA_ROPE, MLA_NOPE + MLA_V, MLA_HEAD_PAD
    w_a = w_in[:, :_KR_OFF].astype(BF16)
    dq_lo = _KR_OFF - MLA_KV_RANK - MLA_Q_RANK
    w_kr = jnp.concatenate(
        [w_in[:, dq_lo:dq_lo + MLA_Q_RANK],
         jnp.pad(w_in[:, _KR_OFF:_KR_OFF + MLA_ROPE], ((0, 0), (MLA_NOPE, hp - hq)))], axis=1).astype(BF16)
    w_g = w_in[:, _KR_OFF + MLA_ROPE:].astype(BF16)
    wuq_pt = w_uq.astype(BF16).T
    kv3 = w_ukv.reshape(w_ukv.shape[0], MLA_HEADS, hkv)
    wuk_p = kv3[:, :, :MLA_NOPE].reshape(-1, MLA_HEADS * MLA_NOPE).astype(BF16)
    wuvt = kv3[:, :, MLA_NOPE:].reshape(-1, MLA_HEADS * MLA_V).astype(BF16).T
    return (w_a, w_kr, w_g), wuq_pt, wuk_p, wuvt


def kernel(x, c, ctx, c_ctx, w_ada, b_ada, w_in, ret_decay_f, ret_decay_b, w_ret_o, mla_q_norm, w_uq,
           mla_kv_norm, w_ukv, w_mla_o, w_out, ln1_g, ln1_b, w_gu, w_down, ln2_g, ln2_b):
    B, L, D = x.shape
    assert w_ada.shape[0] == DEPTH == 1
    i = 0

    c_rows = jnp.concatenate([c, c_ctx[None, :], jnp.zeros((8 - B - 1, D), F32)], axis=0)
    mod = _ada(c_rows, w_ada[i], b_ada[i]).reshape(8, 6, D)
    mod_l = mod_c = mod

    w_groups, wuq_p, wuk_p, wuvt = _prep_weights(w_in[i], w_uq[i], w_ukv[i])
    qn = mla_q_norm[i].reshape(1, -1)
    kvn = mla_kv_norm[i].reshape(1, -1)
    tables = _rope_tables(L)

    rq, rk, rv, srg, sgr, sgm, qm, km, vm = _in_proj(
        x, mod_l, w_groups, qn, wuq_p, kvn, wuk_p, wuvt, tables, is_ctx=False, tm=256)
    rk_c, rv_c, km_c, vm_c = _in_proj(
        ctx, mod_c, w_groups, None, None, kvn, wuk_p, wuvt, None, is_ctx=True, tm=256)

    dec = jnp.stack([jnp.broadcast_to(ret_decay_f[i][:, None], (RET_HEADS, LANES)),
                     jnp.broadcast_to(ret_decay_b[i][:, None], (RET_HEADS, LANES))], axis=1)
    o_ret = _retention(dec, rq, rk, rv, rk_c, rv_c)
    att = _attention(qm, km, vm, km_c, vm_c, tq=1024)

    x1 = _merge(x, mod_l, o_ret, srg, att, sgr, sgm,
                w_ret_o[i].astype(BF16), w_mla_o[i].astype(BF16), w_out[i].astype(BF16),
                ln1_g[i].reshape(1, D), ln1_b[i].reshape(1, D), tm=1024)
    return _ffn(x1, mod_l, w_gu[i].astype(BF16), w_down[i].astype(BF16),
                ln2_g[i].reshape(1, D), ln2_b[i].reshape(1, D), tm=512)
```

```python
import functools
import math

import numpy as np
import jax
import jax.numpy as jnp
from jax import lax
from jax.experimental import pallas as pl
from jax.experimental.pallas import tpu as pltpu

F32 = jnp.float32
BF16 = jnp.bfloat16

D_MODEL = 1024
GRID_W = 64
RET_HEADS = 4
RET_QK = 128
RET_V = 256
RET_CHUNK = 128
RET_ROPE_BASE = 10000.0
MLA_HEADS = 8
MLA_NOPE = 64
MLA_ROPE = 32
MLA_V = 64
MLA_Q_RANK = 384
MLA_KV_RANK = 256
ROPE_BASE = 10000.0
D_FF = -(-8 * D_MODEL // (3 * 256)) * 256
LN_EPS = 1e-5
RMS_EPS = 1e-6
DEPTH = 1
ALPHA = (2.0 * DEPTH) ** 0.25

LANES = 128
MLA_HEAD_PAD = LANES
VMEM_LIMIT = 56 * 1024 * 1024

_RQ = RET_HEADS * RET_QK
_RV = RET_HEADS * RET_V
_KR_OFF = 2 * _RQ + 2 * _RV + MLA_Q_RANK + MLA_KV_RANK
_COLS = {
    "rq": (0, 0, _RQ),
    "rk": (0, _RQ, 2 * _RQ),
    "rv": (0, 2 * _RQ, 2 * _RQ + _RV),
    "rg": (0, 2 * _RQ + _RV, 2 * _RQ + 2 * _RV),
    "dkv": (0, 2 * _RQ + 2 * _RV + MLA_Q_RANK, _KR_OFF),
    "dq": (1, 0, MLA_Q_RANK),
    "kr": (1, MLA_Q_RANK, MLA_Q_RANK + LANES),
    "gr": (2, 0, D_MODEL),
    "gm": (2, D_MODEL, 2 * D_MODEL),
}

_RH = MLA_ROPE // 4


def _dot(a, b):
    return jnp.dot(a, b, preferred_element_type=F32)


def _dot_nt(a, b):
    return lax.dot_general(a, b, (((1,), (1,)), ((), ())), preferred_element_type=F32)


def _dot_tn(a, b):
    return lax.dot_general(a, b, (((0,), (0,)), ((), ())), preferred_element_type=F32)


def _sigmoid(x):
    return 1.0 / (1.0 + jnp.exp(-x))


def _silu(x):
    return x * _sigmoid(x)


def _layer_norm(x, g, b):
    mu = jnp.mean(x, axis=-1, keepdims=True)
    xc = x - mu
    var = jnp.mean(xc * xc, axis=-1, keepdims=True)
    return xc * lax.rsqrt(var + LN_EPS) * g + b


def _rope_half_roll(x, cos, sin_signed):
    return x * cos + pltpu.roll(x, LANES // 2, 1) * sin_signed


def _ada_kernel(c_ref, w_ref, b_ref, o_ref):
    sc = _silu(c_ref[...]).astype(BF16)
    o_ref[...] = _dot(sc, w_ref[...].astype(BF16)) + b_ref[...]


def _ada(c_rows, w_ada, b_ada):
    rows, d = c_rows.shape
    n = w_ada.shape[1]
    tn = 1536
    return pl.pallas_call(
        _ada_kernel,
        grid=(n // tn,),
        in_specs=[pl.BlockSpec((rows, d), lambda j: (0, 0)),
                  pl.BlockSpec((d, tn), lambda j: (0, j)),
                  pl.BlockSpec((1, tn), lambda j: (0, j))],
        out_specs=pl.BlockSpec((rows, tn), lambda j: (0, j)),
        out_shape=jax.ShapeDtypeStruct((rows, n), F32),
        compiler_params=pltpu.CompilerParams(dimension_semantics=("arbitrary",), vmem_limit_bytes=VMEM_LIMIT),
        name="ada",
    )(c_rows, w_ada, b_ada.reshape(1, n))


def _in_proj_kernel(*refs, is_ctx, q_scale):
    if is_ctx:
        (x_ref, mod_ref, wa_ref, wkr_ref, wg_ref, kvn_ref, wuk_ref, wuvt_ref,
         rkt_ref, rv_ref, km_ref, vmt_ref) = refs
    else:
        (x_ref, mod_ref, wa_ref, wkr_ref, wg_ref, qn_ref, wuqt_ref, kvn_ref, wuk_ref, wuvt_ref,
         cr_ref, sr_ref, crt_ref, srt_ref, cm_ref, sma_ref, smb_ref, tqt_ref,
         rq_ref, rkt_ref, rv_ref, srg_ref, sgr_ref, sgm_ref, qmt_ref, km_ref, vmt_ref) = refs

    x = x_ref[0]
    sh1 = mod_ref[0, 0:1, :]
    s1 = mod_ref[0, 1:2, :]
    h = (x * (1.0 + s1) + sh1).astype(BF16)

    def proj(name):
        group, lo, hi = _COLS[name]
        return _dot(h, (wa_ref, wkr_ref, wg_ref)[group][:, lo:hi])

    def rms(v, g):
        return v * lax.rsqrt(jnp.mean(v * v, axis=-1, keepdims=True) + RMS_EPS) * g

    k_scale = RET_QK ** -0.5
    dkv_raw = proj("dkv")
    if is_ctx:
        kr = proj("kr")
    else:
        dq_kr = _dot(h, wkr_ref[...])
        dq_raw, kr = dq_kr[:, :MLA_Q_RANK], dq_kr[:, MLA_Q_RANK:]
    rkt = proj("rk").T
    if not is_ctx:
        rq = proj("rq")

    dkv = rms(dkv_raw, kvn_ref[...]).astype(BF16)
    kv = _dot(dkv, wuk_ref[...])
    vmt_ref[0] = _dot_nt(wuvt_ref[...], dkv).astype(BF16)
    if not is_ctx:
        dq = rms(dq_raw, qn_ref[...]).astype(BF16)
        qt = _dot_nt(wuqt_ref[...], dq)
        srg_ref[0] = _silu(proj("rg")).astype(BF16)
        sgr_ref[0] = _sigmoid(proj("gr")).astype(BF16)
        sgm_ref[0] = _sigmoid(proj("gm")).astype(BF16)
    rv_ref[0] = proj("rv").astype(BF16)

    if is_ctx:
        rkt_ref[0] = (rkt * k_scale).astype(BF16)
    else:
        cr = cr_ref[...]
        sr = sr_ref[...]
        crt = crt_ref[...]
        srt = srt_ref[...]
        half = RET_QK // 2
        for hh in range(RET_HEADS):
            sl = slice(hh * RET_QK, (hh + 1) * RET_QK)
            rq_ref[0, :, sl] = _rope_half_roll(rq[:, sl], cr, sr).astype(BF16)
            x1 = rkt[hh * RET_QK:hh * RET_QK + half]
            x2 = rkt[hh * RET_QK + half:(hh + 1) * RET_QK]
            rkt_ref[0, hh * RET_QK:hh * RET_QK + half, :] = ((x1 * crt - x2 * srt) * k_scale).astype(BF16)
            rkt_ref[0, hh * RET_QK + half:(hh + 1) * RET_QK, :] = ((x2 * crt + x1 * srt) * k_scale).astype(BF16)
        kr = (kr * cm_ref[...] + pltpu.roll(kr, LANES - _RH, 1) * sma_ref[...]
              + pltpu.roll(kr, _RH, 1) * smb_ref[...])
        tqt = tqt_ref[...]
        cos_ax = (tqt[0:_RH], tqt[2 * _RH:3 * _RH])
        sin_ax = (tqt[_RH:2 * _RH], tqt[3 * _RH:4 * _RH])
    nope_lane = lax.broadcasted_iota(jnp.int32, kr.shape, 1) < MLA_NOPE
    for hh in range(MLA_HEADS):
        sl = slice(hh * MLA_HEAD_PAD, (hh + 1) * MLA_HEAD_PAD)
        pair = kv[:, (hh // 2) * LANES:(hh // 2 + 1) * LANES]
        if hh % 2:
            pair = pltpu.roll(pair, LANES - MLA_NOPE, 1)
        km_ref[0, :, sl] = jnp.where(nope_lane, pair, kr).astype(BF16)
        if not is_ctx:
            q0 = hh * (MLA_NOPE + MLA_ROPE)
            slabs = [qt[q0:q0 + MLA_NOPE]]
            for ax in range(2):
                lo = q0 + MLA_NOPE + 2 * ax * _RH
                x1, x2 = qt[lo:lo + _RH], qt[lo + _RH:lo + 2 * _RH]
                slabs += [x1 * cos_ax[ax] - x2 * sin_ax[ax], x2 * cos_ax[ax] + x1 * sin_ax[ax]]
            slabs.append(jnp.zeros((MLA_HEAD_PAD - MLA_NOPE - MLA_ROPE, qt.shape[1]), F32))
            qmt_ref[0, sl, :] = (jnp.concatenate(slabs, axis=0) * q_scale).astype(BF16)


def _in_proj(x, mod, w_groups, qn, wuq_p, kvn, wuk_p, wuvt, tables, *, is_ctx, tm):
    B, L, D = x.shape
    w_specs = [pl.BlockSpec(w.shape, lambda b, i: (0, 0), pipeline_mode=pl.Buffered(1)) for w in w_groups]
    q_scale = (MLA_NOPE + MLA_ROPE) ** -0.5 * math.log2(math.e)
    kw = MLA_HEADS * MLA_HEAD_PAD
    vw = MLA_HEADS * MLA_V

    def const(shape):
        return pl.BlockSpec(shape, lambda b, i: (0,) * len(shape), pipeline_mode=pl.Buffered(1))

    def rows(width):
        return pl.BlockSpec((1, tm, width), lambda b, i: (b, i, 0))

    def out(width):
        return jax.ShapeDtypeStruct((B, L, width), BF16)

    x_spec = rows(D)
    def cols(height):
        return pl.BlockSpec((1, height, tm), lambda b, i: (b, 0, i))

    def out_t(height):
        return jax.ShapeDtypeStruct((B, height, L), BF16)

    if is_ctx:
        mod_spec = pl.BlockSpec((1, 6, D), lambda b, i: (B, 0, 0))
        in_specs = [x_spec, mod_spec, *w_specs, const(kvn.shape), const(wuk_p.shape), const(wuvt.shape)]
        args = (x, mod, *w_groups, kvn, wuk_p, wuvt)
        out_specs = [cols(_RQ), rows(_RV), rows(kw), cols(vw)]
        out_shape = [out_t(_RQ), out(_RV), out(kw), out_t(vw)]
    else:
        mod_spec = pl.BlockSpec((1, 6, D), lambda b, i: (b, 0, 0))
        tab = pl.BlockSpec((tm, LANES), lambda b, i: (i, 0))
        tab_t = pl.BlockSpec((RET_QK // 2, tm), lambda b, i: (0, i))
        tab_qt = pl.BlockSpec((MLA_ROPE, tm), lambda b, i: (0, i))
        in_specs = [x_spec, mod_spec, *w_specs, const(qn.shape), const(wuq_p.shape),
                    const(kvn.shape), const(wuk_p.shape), const(wuvt.shape),
                    tab, tab, tab_t, tab_t, tab, tab, tab, tab_qt]
        args = (x, mod, *w_groups, qn, wuq_p, kvn, wuk_p, wuvt) + tuple(tables)
        out_specs = [rows(_RQ), cols(_RQ), rows(_RV), rows(_RV), rows(D), rows(D), cols(kw), rows(kw), cols(vw)]
        out_shape = [out(_RQ), out_t(_RQ), out(_RV), out(_RV), out(D), out(D), out_t(kw), out(kw), out_t(vw)]
    return pl.pallas_call(
        functools.partial(_in_proj_kernel, is_ctx=is_ctx, q_scale=q_scale),
        grid=(B, L // tm),
        in_specs=in_specs,
        out_specs=out_specs,
        out_shape=out_shape,
        compiler_params=pltpu.CompilerParams(dimension_semantics=("arbitrary", "arbitrary"),
                                             vmem_limit_bytes=VMEM_LIMIT),
        name="in_proj_ctx" if is_ctx else "in_proj",
    )(*args)


def _log_sigmoid(x):
    return jnp.minimum(x, 0.0) - jnp.log(1.0 + jnp.exp(-jnp.abs(x)))


RET_LOOKAHEAD = 2


def _retention_kernel(dec_ref, q_ref, kt_ref, v_ref, kct_ref, vc_ref, o_ref, u_ref, st_ref, *, n_lat, n_ctx):
    C = RET_CHUNK
    lgf = _log_sigmoid(dec_ref[0, 0:1, :])
    lgb = _log_sigmoid(dec_ref[0, 1:2, :])
    ii = lax.broadcasted_iota(jnp.int32, (C, C), 0).astype(F32)
    jj = lax.broadcasted_iota(jnp.int32, (C, C), 1).astype(F32)
    rel = ii - jj
    mask = jnp.where(rel > 0, jnp.exp(rel * lgf), jnp.where(rel < 0, jnp.exp(-rel * lgb), 2.0))
    pos_col = ii[:, 0:1]
    pos_row = jj[0:1, :]
    lgf1 = lgf[:, 0:1]
    lgb1 = lgb[:, 0:1]
    qdec_f = jnp.exp((pos_col + 1.0) * lgf1)
    qdec_b = jnp.exp((C - pos_col) * lgb1)
    kdec_f = jnp.exp((C - 1.0 - pos_row) * lgf)
    kdec_b = jnp.exp(pos_row * lgb)
    cdec_f = jnp.exp(C * lgf1)
    cdec_b = jnp.exp(C * lgb1)

    def increment(ktref, vref, n):
        kt = ktref[0, :, n * C:(n + 1) * C].astype(F32)
        lhs = jnp.concatenate([(kt * kdec_f).astype(BF16), (kt * kdec_b).astype(BF16)], axis=0)
        return _dot(lhs, vref[0, n * C:(n + 1) * C, :])

    zero = jnp.zeros((RET_QK, RET_V), F32)
    s_f, s_b = zero, zero
    u_ctx = [increment(kct_ref, vc_ref, n) for n in range(n_ctx)]
    for n in range(n_ctx):
        s_f = s_f * cdec_f + u_ctx[n][0:RET_QK]
        s_b = s_b * cdec_b + u_ctx[n_ctx - 1 - n][RET_QK:]

    for n in range(n_lat):
        u_ref[n] = increment(kt_ref, v_ref, n)

    for t in range(n_lat):
        nb = n_lat - 1 - t
        st_ref[t, 0:RET_QK, :] = s_f.astype(BF16)
        st_ref[nb, RET_QK:, :] = s_b.astype(BF16)
        s_f = s_f * cdec_f + u_ref[t, 0:RET_QK, :]
        s_b = s_b * cdec_b + u_ref[nb, RET_QK:, :]

    scores = {}

    def score(n):
        scores[n] = _dot(q_ref[0, n * C:(n + 1) * C, :], kt_ref[0, :, n * C:(n + 1) * C])

    def consume(n):
        rows = slice(n * C, (n + 1) * C)
        a = (scores.pop(n) * mask).astype(BF16)
        q = q_ref[0, rows, :].astype(F32)
        q_dec = jnp.concatenate([(q * qdec_f).astype(BF16), (q * qdec_b).astype(BF16)], axis=1)
        o = _dot(a, v_ref[0, rows, :]) + _dot(q_dec, st_ref[n])
        mu = jnp.mean(o, axis=-1, keepdims=True)
        oc = o - mu
        var = jnp.mean(oc * oc, axis=-1, keepdims=True)
        o_ref[0, rows, :] = (oc * lax.rsqrt(var + LN_EPS)).astype(BF16)

    for n in range(n_lat + RET_LOOKAHEAD):
        if n < n_lat:
            score(n)
        if n >= RET_LOOKAHEAD:
            consume(n - RET_LOOKAHEAD)


def _retention(dec, rq, rkt, rv, rkt_c, rv_c):
    B, L, _ = rq.shape
    Lc = rv_c.shape[1]
    n_lat, n_ctx = L // RET_CHUNK, Lc // RET_CHUNK
    return pl.pallas_call(
        functools.partial(_retention_kernel, n_lat=n_lat, n_ctx=n_ctx),
        grid=(B, RET_HEADS),
        in_specs=[pl.BlockSpec((1, 2, LANES), lambda b, h: (h, 0, 0)),
                  pl.BlockSpec((1, L, RET_QK), lambda b, h: (b, 0, h)),
                  pl.BlockSpec((1, RET_QK, L), lambda b, h: (b, h, 0)),
                  pl.BlockSpec((1, L, RET_V), lambda b, h: (b, 0, h)),
                  pl.BlockSpec((1, RET_QK, Lc), lambda b, h: (b, h, 0)),
                  pl.BlockSpec((1, Lc, RET_V), lambda b, h: (b, 0, h))],
        out_specs=pl.BlockSpec((1, L, RET_V), lambda b, h: (b, 0, h)),
        out_shape=jax.ShapeDtypeStruct((B, L, RET_HEADS * RET_V), BF16),
        scratch_shapes=[pltpu.VMEM((n_lat, 2 * RET_QK, RET_V), F32),
                        pltpu.VMEM((n_lat, 2 * RET_QK, RET_V), BF16)],
        compiler_params=pltpu.CompilerParams(dimension_semantics=("arbitrary", "arbitrary"),
                                             vmem_limit_bytes=VMEM_LIMIT),
        name="retention",
    )(dec, rq, rkt, rv, rkt_c, rv_c)


ATT_SCORE_KEYS = 512
ATT_KEY_TILE = 256
ATT_Q_COLS = 256
ATT_ONES_ROWS = 16
ATT_MAX_ROWS = 64
ATT_LOOKAHEAD = 6


def _col_max(s):
    part = s[0:ATT_MAX_ROWS]
    for r in range(ATT_MAX_ROWS, s.shape[0], ATT_MAX_ROWS):
        part = jnp.maximum(part, s[r:r + ATT_MAX_ROWS])
    return jnp.max(part, axis=0, keepdims=True)


def _attention_kernel(q_ref, k_ref, vt_ref, kc_ref, vtc_ref, o_ref, *, l_lat, l_ctx):
    groups = [(kc_ref, vtc_ref, lo, min(ATT_SCORE_KEYS, l_ctx - lo)) for lo in range(0, l_ctx, ATT_SCORE_KEYS)]
    groups += [(k_ref, vt_ref, lo, ATT_SCORE_KEYS) for lo in range(0, l_lat, ATT_SCORE_KEYS)]
    tq = q_ref.shape[2]
    streams = [(hh, qb) for hh in range(2) for qb in range(tq // ATT_Q_COLS)]
    hs = [slice(hh * MLA_HEAD_PAD, (hh + 1) * MLA_HEAD_PAD) for hh in range(2)]
    vs = [slice(hh * MLA_V, (hh + 1) * MLA_V) for hh in range(2)]
    qs = [slice(qb * ATT_Q_COLS, (qb + 1) * ATT_Q_COLS) for qb in range(tq // ATT_Q_COLS)]
    m = {st: None for st in streams}
    acc = {st: None for st in streams}
    items = [(g, st, off, min(ATT_KEY_TILE, grp[3] - off))
             for g, grp in enumerate(groups) for st in streams for off in range(0, grp[3], ATT_KEY_TILE)]
    scores = {}

    def score(g, st):
        kref, _, lo, n = groups[g]
        hh, qb = st
        scores[g, st] = _dot(kref[0, lo:lo + n, hs[hh]], q_ref[0, hs[hh], qs[qb]])

    def consume(g, st, off, n):
        _, vref, lo, _ = groups[g]
        hh, _ = st
        s = scores[g, st][off:off + n]
        t_max = _col_max(s)
        m_new = t_max if m[st] is None else jnp.maximum(m[st], t_max)
        p = jnp.exp2(s - m_new).astype(BF16)
        ones = jnp.ones((ATT_ONES_ROWS, n), BF16)
        part = _dot(jnp.concatenate([vref[0, vs[hh], lo + off:lo + off + n], ones], axis=0), p)
        acc[st] = part if m[st] is None else acc[st] * jnp.exp2(m[st] - m_new) + part
        m[st] = m_new

    for i in range(len(items) + ATT_LOOKAHEAD):
        if i < len(items) and items[i][2] == 0:
            score(items[i][0], items[i][1])
        if i >= ATT_LOOKAHEAD:
            consume(*items[i - ATT_LOOKAHEAD])
    for hh, qb in streams:
        a = acc[hh, qb]
        o_ref[0, vs[hh], qs[qb]] = (a[0:MLA_V] / a[MLA_V:MLA_V + 1]).astype(BF16)


def _attention(qmt, km, vmt, km_c, vmt_c, *, tq):
    B, L, _ = km.shape
    Lc = km_c.shape[1]
    assert L % ATT_SCORE_KEYS == 0 and ATT_SCORE_KEYS % ATT_KEY_TILE == 0 and Lc % LANES == 0
    pairs = MLA_HEADS // 2
    qk_w = 2 * MLA_HEAD_PAD
    v_w = 2 * MLA_V
    return pl.pallas_call(
        functools.partial(_attention_kernel, l_lat=L, l_ctx=Lc),
        grid=(B, pairs, L // tq),
        in_specs=[pl.BlockSpec((1, qk_w, tq), lambda b, p, i: (b, p, i)),
                  pl.BlockSpec((1, L, qk_w), lambda b, p, i: (b, 0, p)),
                  pl.BlockSpec((1, v_w, L), lambda b, p, i: (b, p, 0)),
                  pl.BlockSpec((1, Lc, qk_w), lambda b, p, i: (b, 0, p)),
                  pl.BlockSpec((1, v_w, Lc), lambda b, p, i: (b, p, 0))],
        out_specs=pl.BlockSpec((1, v_w, tq), lambda b, p, i: (b, p, i)),
        out_shape=jax.ShapeDtypeStruct((B, MLA_HEADS * MLA_V, L), BF16),
        compiler_params=pltpu.CompilerParams(dimension_semantics=("arbitrary", "arbitrary", "arbitrary"),
                                             vmem_limit_bytes=VMEM_LIMIT),
        name="attention",
    )(qmt, km, vmt, km_c, vmt_c)


ROW_SUBTILE = 256


def _row_subtiles(rows):
    return [slice(lo, lo + ROW_SUBTILE) for lo in range(0, rows, ROW_SUBTILE)]


def _merge_kernel(x_ref, mod_ref, o_ret_ref, srg_ref, att_ref, sgr_ref, sgm_ref,
                  wro_ref, wmo_ref, wout_ref, g_ref, b_ref, y_ref):
    g1 = mod_ref[0, 2:3, :]
    subs = _row_subtiles(x_ref.shape[1])
    branch = []
    for r in subs:
        ret_in = srg_ref[0, r, :] * o_ret_ref[0, r, :]
        ret = _dot(ret_in, wro_ref[...])
        mla = _dot_tn(att_ref[0, :, r], wmo_ref[...])
        branch.append((ret, mla))
    for r, (ret, mla) in zip(subs, branch):
        mix = (sgr_ref[0, r, :].astype(F32) * ret + sgm_ref[0, r, :].astype(F32) * mla).astype(BF16)
        y = _dot(mix, wout_ref[...])
        y_ref[0, r, :] = _layer_norm(ALPHA * x_ref[0, r, :] + g1 * y, g_ref[...], b_ref[...])


def _merge(x, mod, o_ret, srg, att, sgr, sgm, wro, wmo, wout, ln_g, ln_b, *, tm):
    B, L, D = x.shape

    def const(shape):
        return pl.BlockSpec(shape, lambda b, i: (0,) * len(shape), pipeline_mode=pl.Buffered(1))

    def rows(width):
        return pl.BlockSpec((1, tm, width), lambda b, i: (b, i, 0))

    return pl.pallas_call(
        _merge_kernel,
        grid=(B, L // tm),
        in_specs=[rows(D), pl.BlockSpec((1, 6, D), lambda b, i: (b, 0, 0)),
                  rows(o_ret.shape[2]), rows(srg.shape[2]),
                  pl.BlockSpec((1, att.shape[1], tm), lambda b, i: (b, 0, i)), rows(D), rows(D),
                  const(wro.shape), const(wmo.shape), const(wout.shape), const(ln_g.shape), const(ln_b.shape)],
        out_specs=rows(D),
        out_shape=jax.ShapeDtypeStruct((B, L, D), F32),
        compiler_params=pltpu.CompilerParams(dimension_semantics=("arbitrary", "arbitrary"),
                                             vmem_limit_bytes=VMEM_LIMIT),
        name="merge",
    )(x, mod, o_ret, srg, att, sgr, sgm, wro, wmo, wout, ln_g, ln_b)


def _ffn_kernel(x_ref, mod_ref, wgu_ref, wdn_ref, g_ref, b_ref, y_ref):
    sh2 = mod_ref[0, 3:4, :]
    s2 = mod_ref[0, 4:5, :]
    g2 = mod_ref[0, 5:6, :]
    subs = _row_subtiles(x_ref.shape[1])

    def gate_up(r):
        h = (x_ref[0, r, :] * (1.0 + s2) + sh2).astype(BF16)
        return _dot(h, wgu_ref[:, 0:D_FF]), _dot(h, wgu_ref[:, D_FF:2 * D_FF])

    def down(r, ab):
        f = _dot((_silu(ab[0]) * ab[1]).astype(BF16), wdn_ref[...])
        y_ref[0, r, :] = _layer_norm(ALPHA * x_ref[0, r, :] + g2 * f, g_ref[...], b_ref[...])

    pending = gate_up(subs[0])
    for k in range(1, len(subs)):
        nxt = gate_up(subs[k])
        down(subs[k - 1], pending)
        pending = nxt
    down(subs[-1], pending)


def _ffn(x, mod, wgu, wdn, ln_g, ln_b, *, tm):
    B, L, D = x.shape

    def const(shape):
        return pl.BlockSpec(shape, lambda b, i: (0,) * len(shape), pipeline_mode=pl.Buffered(1))

    rows = pl.BlockSpec((1, tm, D), lambda b, i: (b, i, 0))
    return pl.pallas_call(
        _ffn_kernel,
        grid=(B, L // tm),
        in_specs=[rows, pl.BlockSpec((1, 6, D), lambda b, i: (b, 0, 0)),
                  const(wgu.shape), const(wdn.shape), const(ln_g.shape), const(ln_b.shape)],
        out_specs=rows,
        out_shape=jax.ShapeDtypeStruct((B, L, D), F32),
        compiler_params=pltpu.CompilerParams(dimension_semantics=("arbitrary", "arbitrary"),
                                             vmem_limit_bytes=VMEM_LIMIT),
        name="ffn",
    )(x, mod, wgu, wdn, ln_g, ln_b)


def _rope_tables(L):
    t = np.arange(L, dtype=np.float64)
    half = RET_QK // 2
    fr = RET_ROPE_BASE ** (-np.arange(half, dtype=np.float64) / half)
    ang = t[:, None] * fr[None, :]
    cos_r = np.concatenate([np.cos(ang), np.cos(ang)], -1)
    sin_r = np.concatenate([-np.sin(ang), np.sin(ang)], -1)

    row = np.floor(t / GRID_W)
    col = t - row * GRID_W
    fm = ROPE_BASE ** (-np.arange(_RH, dtype=np.float64) / _RH)
    cos_t = np.ascontiguousarray(np.cos(ang).T)
    sin_t = np.ascontiguousarray(np.sin(ang).T)

    ang_ax = (row[:, None] * fm[None, :], col[:, None] * fm[None, :])
    cos_k = np.ones((L, LANES))
    sin_up = np.zeros((L, LANES))
    sin_down = np.zeros((L, LANES))
    for ax in range(2):
        lo = MLA_NOPE + 2 * ax * _RH
        cos_k[:, lo:lo + 2 * _RH] = np.concatenate([np.cos(ang_ax[ax])] * 2, -1)
        sin_up[:, lo:lo + _RH] = -np.sin(ang_ax[ax])
        sin_down[:, lo + _RH:lo + 2 * _RH] = np.sin(ang_ax[ax])
    tab_qt = np.ascontiguousarray(np.concatenate(
        [np.cos(ang_ax[0]), np.sin(ang_ax[0]), np.cos(ang_ax[1]), np.sin(ang_ax[1])], -1).T)
    return tuple(jnp.asarray(a, F32) for a in (cos_r, sin_r, cos_t, sin_t, cos_k, sin_up, sin_down, tab_qt))


def _prep_weights(w_in, w_uq, w_ukv):
    hq, hkv, hp = MLA_NOPE + MLA_ROPE, MLA_NOPE + MLA_V, MLA_HEAD_PAD
    w_a = w_in[:, :_KR_OFF].astype(BF16)
    dq_lo = _KR_OFF - MLA_KV_RANK - MLA_Q_RANK
    w_kr = jnp.concatenate(
        [w_in[:, dq_lo:dq_lo + MLA_Q_RANK],
         jnp.pad(w_in[:, _KR_OFF:_KR_OFF + MLA_ROPE], ((0, 0), (MLA_NOPE, hp - hq)))], axis=1).astype(BF16)
    w_g = w_in[:, _KR_OFF + MLA_ROPE:].astype(BF16)
    wuq_pt = w_uq.astype(BF16).T
    kv3 = w_ukv.reshape(w_ukv.shape[0], MLA_HEADS, hkv)
    wuk_p = kv3[:, :, :MLA_NOPE].reshape(-1, MLA_HEADS * MLA_NOPE).astype(BF16)
    wuvt = kv3[:, :, MLA_NOPE:].reshape(-1, MLA_HEADS * MLA_V).astype(BF16).T
    return (w_a, w_kr, w_g), wuq_pt, wuk_p, wuvt


def kernel(x, c, ctx, c_ctx, w_ada, b_ada, w_in, ret_decay_f, ret_decay_b, w_ret_o, mla_q_norm, w_uq,
           mla_kv_norm, w_ukv, w_mla_o, w_out, ln1_g, ln1_b, w_gu, w_down, ln2_g, ln2_b):
    B, L, D = x.shape
    assert w_ada.shape[0] == DEPTH == 1
    i = 0

    c_rows = jnp.concatenate([c, c_ctx[None, :], jnp.zeros((8 - B - 1, D), F32)], axis=0)
    mod = _ada(c_rows, w_ada[i], b_ada[i]).reshape(8, 6, D)
    mod_l = mod_c = mod

    w_groups, wuq_p, wuk_p, wuvt = _prep_weights(w_in[i], w_uq[i], w_ukv[i])
    qn = mla_q_norm[i].reshape(1, -1)
    kvn = mla_kv_norm[i].reshape(1, -1)
    tables = _rope_tables(L)

    rq, rk, rv, srg, sgr, sgm, qm, km, vm = _in_proj(
        x, mod_l, w_groups, qn, wuq_p, kvn, wuk_p, wuvt, tables, is_ctx=False, tm=256)
    rk_c, rv_c, km_c, vm_c = _in_proj(
        ctx, mod_c, w_groups, None, None, kvn, wuk_p, wuvt, None, is_ctx=True, tm=256)

    dec = jnp.stack([jnp.broadcast_to(ret_decay_f[i][:, None], (RET_HEADS, LANES)),
                     jnp.broadcast_to(ret_decay_b[i][:, None], (RET_HEADS, LANES))], axis=1)
    o_ret = _retention(dec, rq, rk, rv, rk_c, rv_c)
    att = _attention(qm, km, vm, km_c, vm_c, tq=1024)

    x1 = _merge(x, mod_l, o_ret, srg, att, sgr, sgm,
                w_ret_o[i].astype(BF16), w_mla_o[i].astype(BF16), w_out[i].astype(BF16),
                ln1_g[i].reshape(1, D), ln1_b[i].reshape(1, D), tm=1024)
    return _ffn(x1, mod_l, w_gu[i].astype(BF16), w_down[i].astype(BF16),
                ln2_g[i].reshape(1, D), ln2_b[i].reshape(1, D), tm=1024)
```
